```python
import jax
import jax.numpy as jnp
from jax import lax
import numpy as np

D_MODEL = 1024
BATCH = 32
SEQ = 256
DEPTH = 4
DEC_BATCH = 8
DEC_SEQ = 2048
PAST_LEN = 512

N_GROUPS = 4
GROUP_W = D_MODEL // N_GROUPS
HEAD_DIM = 64
CONV_CH = GROUP_W
CONV_WIDTH = 3
SGU_W = GROUP_W
SGU_HEADS = SGU_W // HEAD_DIM
CHUNK = 128
MLA_HEADS = 4
MLA_NOPE = 64
MLA_ROPE = 32
MLA_V = GROUP_W // MLA_HEADS
MLA_Q_LORA = D_MODEL // 4
MLA_KV_LORA = D_MODEL // 8
SWA_HEADS = GROUP_W // HEAD_DIM
SWA_KV_HEADS = 2
SWA_GROUP = SWA_HEADS // SWA_KV_HEADS
SWA_WINDOW = 128
SWA_BLOCK = 128
Q_BLOCK = 128
MLP_HIDDEN = 4 * D_MODEL
GRID_W = 64
ROPE_THETA = 10000.0
EPS = 1e-6
N_MOD = 6
IN_COLS = 3 * CONV_CH + 2 * SGU_W + MLA_Q_LORA + MLA_KV_LORA + MLA_ROPE + (SWA_HEADS + 2 * SWA_KV_HEADS) * HEAD_DIM
MLA_SCALE = (MLA_NOPE + MLA_ROPE) ** -0.5
SWA_SCALE = HEAD_DIM ** -0.5
NEG_INF = -1e30

kernel_name = 'hybrid_flow_prefix_step'


def rmsnorm(x, g):
    xf = x.astype(jnp.float32)
    y = xf * lax.rsqrt(jnp.mean(xf * xf, axis=-1, keepdims=True) + EPS)
    return (y * g.astype(jnp.float32)).astype(x.dtype)


def axial_rope_tables(n_tokens, rot_dim, dtype):
    rows = n_tokens // GRID_W
    row = jnp.repeat(jnp.arange(rows, dtype=jnp.float32), GRID_W)
    col = jnp.tile(jnp.arange(GRID_W, dtype=jnp.float32), rows)
    half = rot_dim // 2
    inv_freq = ROPE_THETA ** (-jnp.arange(0, half, 2, dtype=jnp.float32) / half)
    ang_r = row[:, None] * inv_freq[None, :]
    ang_c = col[:, None] * inv_freq[None, :]
    ang = jnp.concatenate([ang_r, ang_r, ang_c, ang_c], axis=-1)
    return jnp.cos(ang).astype(dtype), jnp.sin(ang).astype(dtype)


def apply_axial_rope(x, cos, sin):
    bshape = (cos.shape[0],) + (1,) * (x.ndim - 3) + (cos.shape[1],)
    cos = cos.reshape(bshape)
    sin = sin.reshape(bshape)
    x1, x2, x3, x4 = jnp.split(x, 4, axis=-1)
    rot = jnp.concatenate([-x2, x1, -x4, x3], axis=-1)
    return x * cos + rot * sin


def split_in_proj(z):
    sizes = (CONV_CH, CONV_CH, CONV_CH, 2 * SGU_W, MLA_Q_LORA, MLA_KV_LORA, MLA_ROPE,
             SWA_HEADS * HEAD_DIM, SWA_KV_HEADS * HEAD_DIM, SWA_KV_HEADS * HEAD_DIM)
    parts = []
    start = 0
    for size in sizes:
        parts.append(z[..., start:start + size])
        start += size
    return parts


def dwconv_centred(x, w):
    return lax.conv_general_dilated(
        x, w[:, None, :].astype(x.dtype), window_strides=(1,),
        padding=((CONV_WIDTH // 2, CONV_WIDTH // 2),),
        dimension_numbers=('NWC', 'WIO', 'NWC'), feature_group_count=x.shape[-1])


def chunk_sgu(uv, norm_g, w_s, b_s):
    z = jax.nn.gelu(uv)
    u, v = jnp.split(z, 2, axis=-1)
    v = rmsnorm(v, norm_g)
    bsz, n, _ = v.shape
    vh = v.reshape(bsz, n // CHUNK, CHUNK, SGU_HEADS, HEAD_DIM)
    sv = jnp.einsum('hpq,bcqhd->bcphd', w_s, vh) + b_s.T[None, None, :, :, None]
    return u * sv.reshape(bsz, n, SGU_W)


def mla_expand(c_kv, k_pe, w_kv_up):
    bsz, n, _ = c_kv.shape
    kv = (c_kv @ w_kv_up).reshape(bsz, n, MLA_HEADS, MLA_NOPE + MLA_V)
    k_nope, v = kv[..., :MLA_NOPE], kv[..., MLA_NOPE:]
    k_rope = jnp.broadcast_to(k_pe[:, :, None, :], (bsz, n, MLA_HEADS, MLA_ROPE))
    return jnp.concatenate([k_nope, k_rope], axis=-1), v


def blocked_attention(q, k, v, scale):
    bsz, n_q, n_h, d_k = q.shape
    qb = q.reshape(bsz, n_q // Q_BLOCK, Q_BLOCK, n_h, d_k).transpose(1, 0, 2, 3, 4)
    kf = k.astype(jnp.float32)

    def one_block(q_blk):
        s = jnp.einsum('bqhd,bkhd->bhqk', q_blk.astype(jnp.float32), kf) * scale
        p = jax.nn.softmax(s, axis=-1)
        return jnp.einsum('bhqk,bkhd->bqhd', p.astype(v.dtype), v)

    out = lax.map(one_block, qb)
    return out.transpose(1, 0, 2, 3, 4).reshape(bsz, n_q, n_h, v.shape[-1])


def sink_column(sink, like):
    col = sink.astype(jnp.float32).reshape(SWA_KV_HEADS, SWA_GROUP)[:, :, None, None]
    return jnp.broadcast_to(col, like.shape[:-1] + (1,))


def gqa_sink_blocked(q, k, v, sink):
    bsz, n_q = q.shape[:2]
    qb = q.reshape(bsz, n_q // Q_BLOCK, Q_BLOCK, SWA_KV_HEADS, SWA_GROUP, HEAD_DIM).transpose(1, 0, 2, 3, 4, 5)
    kf = k.astype(jnp.float32)

    def one_block(q_blk):
        s = jnp.einsum('bqngd,bsnd->bngqs', q_blk.astype(jnp.float32), kf) * SWA_SCALE
        p = jax.nn.softmax(jnp.concatenate([s, sink_column(sink, s)], axis=-1), axis=-1)[..., :-1]
        return jnp.einsum('bngqs,bsnd->bqngd', p.astype(v.dtype), v)

    out = lax.map(one_block, qb)
    return out.transpose(1, 0, 2, 3, 4, 5).reshape(bsz, n_q, SWA_HEADS * HEAD_DIM)


def banded_sink_attention(q, k, v, k_ctx, v_ctx, sink):
    bsz, n = q.shape[:2]
    nb = n // SWA_BLOCK
    qb = q.reshape(bsz, nb, SWA_BLOCK, SWA_KV_HEADS, SWA_GROUP, HEAD_DIM).astype(jnp.float32)

    def band(t):
        tb = t.reshape(bsz, nb, SWA_BLOCK, SWA_KV_HEADS, HEAD_DIM)
        zero = jnp.zeros_like(tb[:, :1])
        prev = jnp.concatenate([zero, tb[:, :-1]], axis=1)
        nxt = jnp.concatenate([tb[:, 1:], zero], axis=1)
        return jnp.concatenate([prev, tb, nxt], axis=2)

    k_band, v_band = band(k), band(v)
    blk = jnp.arange(nb)[:, None] * SWA_BLOCK
    q_pos = blk + jnp.arange(SWA_BLOCK)[None, :]
    k_pos = blk - SWA_BLOCK + jnp.arange(3 * SWA_BLOCK)[None, :]
    kp = k_pos[:, None, :]
    valid = (jnp.abs(q_pos[:, :, None] - kp) <= SWA_WINDOW) & (kp >= 0) & (kp < n)
    s_band = jnp.einsum('bcqngd,bcknd->bcngqk', qb, k_band.astype(jnp.float32)) * SWA_SCALE
    s_band = jnp.where(valid[None, :, None, None, :, :], s_band, NEG_INF)
    s_ctx = jnp.einsum('bcqngd,bsnd->bcngqs', qb, k_ctx.astype(jnp.float32)) * SWA_SCALE
    logits = jnp.concatenate([s_band, s_ctx, sink_column(sink, s_band)], axis=-1)
    p = jax.nn.softmax(logits, axis=-1).astype(v.dtype)
    n_band = 3 * SWA_BLOCK
    out = (jnp.einsum('bcngqk,bcknd->bcqngd', p[..., :n_band], v_band)
           + jnp.einsum('bcngqs,bsnd->bcqngd', p[..., n_band:-1], v_ctx))
    return out.reshape(bsz, n, SWA_HEADS * HEAD_DIM)


def mixer_front(h, p):
    bsz, n, _ = h.shape
    z = h @ p['w_in']
    a_b, a_c, a_x, sgu_uv, cq, ckv, k_pe, sq, sk, sv = split_in_proj(z)
    y_conv = a_b * dwconv_centred(a_c * a_x, p['conv_w'])
    y_sgu = chunk_sgu(sgu_uv, p['sgu_norm'], p['sgu_w'], p['sgu_b'])
    q_mla = (rmsnorm(cq, p['q_norm']) @ p['w_q_up']).reshape(bsz, n, MLA_HEADS, MLA_NOPE + MLA_ROPE)
    c_kv = rmsnorm(ckv, p['kv_norm'])
    q_swa = sq.reshape(bsz, n, SWA_KV_HEADS, SWA_GROUP, HEAD_DIM)
    k_swa = sk.reshape(bsz, n, SWA_KV_HEADS, HEAD_DIM)
    v_swa = sv.reshape(bsz, n, SWA_KV_HEADS, HEAD_DIM)
    return y_conv, y_sgu, q_mla, c_kv, k_pe, q_swa, k_swa, v_swa


def mix_context(h, p):
    bsz, n, _ = h.shape
    y_conv, y_sgu, q_mla, c_kv, k_pe, q_swa, k_swa, v_swa = mixer_front(h, p)
    k_mla, v_mla = mla_expand(c_kv, k_pe, p['w_kv_up'])
    y_mla = blocked_attention(q_mla, k_mla, v_mla, MLA_SCALE).reshape(bsz, n, MLA_HEADS * MLA_V)
    y_swa = gqa_sink_blocked(q_swa, k_swa, v_swa, p['sink'])
    y = jnp.concatenate([y_conv, y_sgu, y_mla, y_swa], axis=-1) @ p['w_out']
    return y, (c_kv, k_pe, k_swa, v_swa)


def mix_latent(h, p, ckv_ctx, kpe_ctx, k_ctx, v_ctx):
    bsz, n, _ = h.shape
    y_conv, y_sgu, q_mla, c_kv, k_pe, q_swa, k_swa, v_swa = mixer_front(h, p)
    cos_m, sin_m = axial_rope_tables(n, MLA_ROPE, h.dtype)
    q_mla = jnp.concatenate([q_mla[..., :MLA_NOPE], apply_axial_rope(q_mla[..., MLA_NOPE:], cos_m, sin_m)], axis=-1)
    k_lat, v_lat = mla_expand(c_kv, apply_axial_rope(k_pe, cos_m, sin_m), p['w_kv_up'])
    k_cm, v_cm = mla_expand(ckv_ctx, kpe_ctx, p['w_kv_up'])
    y_mla = blocked_attention(q_mla, jnp.concatenate([k_lat, k_cm], axis=1),
                              jnp.concatenate([v_lat, v_cm], axis=1), MLA_SCALE).reshape(bsz, n, MLA_HEADS * MLA_V)
    cos_s, sin_s = axial_rope_tables(n, HEAD_DIM, h.dtype)
    y_swa = banded_sink_attention(apply_axial_rope(q_swa, cos_s, sin_s), apply_axial_rope(k_swa, cos_s, sin_s),
                                  v_swa, k_ctx, v_ctx, p['sink'])
    return jnp.concatenate([y_conv, y_sgu, y_mla, y_swa], axis=-1) @ p['w_out']


def squared_relu_mlp(h, w1, w2):
    return jnp.square(jax.nn.relu(h @ w1)) @ w2


def modulate(x, g, shift, scale):
    return rmsnorm(x, g) * (1 + scale) + shift


def trunk_layer(x, mod, p, mix_fn):
    shift1, scale1, gate1, shift2, scale2, gate2 = jnp.split(mod, N_MOD, axis=-1)
    y, ctx_state = mix_fn(modulate(x, p['norm1'], shift1, scale1))
    x = x + gate1 * y
    x = x + gate2 * squared_relu_mlp(modulate(x, p['norm2'], shift2, scale2), p['w1'], p['w2'])
    return x, ctx_state


def setup_inputs(seed: int = 0) -> dict:
    key = jax.random.key(seed)
    ks = jax.random.split(key, 26)

    def nrm(k, shape, scale=1.0):
        return jax.random.normal(k, shape, dtype=jnp.float32) * scale

    def gain(k, shape):
        return 1.0 + 0.02 * jax.random.normal(k, shape, dtype=jnp.float32)

    return {
        'x_prompt': nrm(ks[0], (BATCH, SEQ, D_MODEL)),
        'x_sample': nrm(ks[1], (DEC_BATCH, DEC_SEQ, D_MODEL)),
        'cache_mla_ckv': nrm(ks[2], (DEC_BATCH, DEPTH, PAST_LEN, MLA_KV_LORA)),
        'cache_mla_kpe': nrm(ks[3], (DEC_BATCH, DEPTH, PAST_LEN, MLA_ROPE)),
        'cache_swa_k': nrm(ks[4], (DEC_BATCH, DEPTH, PAST_LEN, SWA_KV_HEADS, HEAD_DIM)),
        'cache_swa_v': nrm(ks[5], (DEC_BATCH, DEPTH, PAST_LEN, SWA_KV_HEADS, HEAD_DIM)),
        'c': nrm(ks[6], (DEC_BATCH, D_MODEL)),
        'c_ctx': nrm(ks[7], (D_MODEL,)),
        'w_ada': nrm(ks[8], (DEPTH, D_MODEL, N_MOD * D_MODEL), 0.5 * D_MODEL ** -0.5),
        'b_ada': nrm(ks[9], (DEPTH, N_MOD * D_MODEL), 0.01),
        'norm1': gain(ks[10], (DEPTH, D_MODEL)),
        'norm2': gain(ks[11], (DEPTH, D_MODEL)),
        'w_in': nrm(ks[12], (DEPTH, D_MODEL, IN_COLS), D_MODEL ** -0.5),
        'conv_w': nrm(ks[13], (DEPTH, CONV_WIDTH, CONV_CH), CONV_WIDTH ** -0.5),
        'sgu_norm': gain(ks[14], (DEPTH, SGU_W)),
        'sgu_w': nrm(ks[15], (DEPTH, SGU_HEADS, CHUNK, CHUNK), CHUNK ** -0.5),
        'sgu_b': nrm(ks[16], (DEPTH, SGU_HEADS, CHUNK), 0.02),
        'mla_q_norm': gain(ks[17], (DEPTH, MLA_Q_LORA)),
        'mla_w_q_up': nrm(ks[18], (DEPTH, MLA_Q_LORA, MLA_HEADS * (MLA_NOPE + MLA_ROPE)), MLA_Q_LORA ** -0.5),
        'mla_kv_norm': gain(ks[19], (DEPTH, MLA_KV_LORA)),
        'mla_w_kv_up': nrm(ks[20], (DEPTH, MLA_KV_LORA, MLA_HEADS * (MLA_NOPE + MLA_V)), MLA_KV_LORA ** -0.5),
        'swa_sink': nrm(ks[21], (DEPTH, SWA_HEADS)),
        'w_out': nrm(ks[22], (DEPTH, D_MODEL, D_MODEL), D_MODEL ** -0.5),
        'mlp_w1': nrm(ks[23], (DEPTH, D_MODEL, MLP_HIDDEN), D_MODEL ** -0.5),
        'mlp_w2': nrm(ks[24], (DEPTH, MLP_HIDDEN, D_MODEL), MLP_HIDDEN ** -0.5),
        'final_norm': gain(ks[25], (D_MODEL,)),
    }


def reference(x_prompt, x_sample, cache_mla_ckv, cache_mla_kpe, cache_swa_k, cache_swa_v, c, c_ctx,
              w_ada, b_ada, norm1, norm2, w_in, conv_w, sgu_norm, sgu_w, sgu_b, mla_q_norm, mla_w_q_up,
              mla_kv_norm, mla_w_kv_up, swa_sink, w_out, mlp_w1, mlp_w2, final_norm):
    xp = x_prompt
    xs = x_sample
    ckv_list, kpe_list, k_list, v_list = [], [], [], []
    for l in range(DEPTH):
        p = {'norm1': norm1[l], 'norm2': norm2[l], 'w_in': w_in[l], 'conv_w': conv_w[l],
             'sgu_norm': sgu_norm[l], 'sgu_w': sgu_w[l], 'sgu_b': sgu_b[l], 'q_norm': mla_q_norm[l],
             'w_q_up': mla_w_q_up[l], 'kv_norm': mla_kv_norm[l], 'w_kv_up': mla_w_kv_up[l],
             'sink': swa_sink[l], 'w_out': w_out[l], 'w1': mlp_w1[l], 'w2': mlp_w2[l]}
        mod_ctx = (jax.nn.silu(c_ctx) @ w_ada[l] + b_ada[l])[None, None, :]
        mod_lat = (jax.nn.silu(c) @ w_ada[l] + b_ada[l])[:, None, :]
        xp, (ckv_l, kpe_l, k_l, v_l) = trunk_layer(xp, mod_ctx, p, lambda h, p=p: mix_context(h, p))
        ckv_list.append(ckv_l)
        kpe_list.append(kpe_l)
        k_list.append(k_l)
        v_list.append(v_l)
        xs, _ = trunk_layer(
            xs, mod_lat, p,
            lambda h, p=p, l=l: (mix_latent(h, p, cache_mla_ckv[:, l], cache_mla_kpe[:, l],
                                            cache_swa_k[:, l], cache_swa_v[:, l]), None))
    y_prompt = rmsnorm(xp, final_norm)
    y_sample = rmsnorm(xs, final_norm)
    new_mla_ckv = jnp.stack(ckv_list, axis=1)
    new_mla_kpe = jnp.stack(kpe_list, axis=1)
    new_swa_k = jnp.stack(k_list, axis=1)
    new_swa_v = jnp.stack(v_list, axis=1)
    return (y_prompt, y_sample, new_mla_ckv, new_mla_kpe, new_swa_k, new_swa_v)
```

```python
import functools

import jax
import jax.numpy as jnp
from jax import lax
from jax.experimental import pallas as pl
from jax.experimental.pallas import tpu as pltpu

D_MODEL = 1024
GROUP_W = 256
HEAD_DIM = 64
CONV_WIDTH = 3
CHUNK = 128
SGU_HEADS = 4
MLA_HEADS = 4
MLA_NOPE = 64
MLA_ROPE = 32
MLA_V = 64
MLA_Q_LORA = 256
MLA_KV_LORA = 128
SWA_KV_HEADS = 2
SWA_GROUP = 2
SWA_WINDOW = 128
MLP_HIDDEN = 4096
GRID_W = 64
ROPE_THETA = 10000.0
EPS = 1e-6
N_MOD = 6
MLA_SCALE = (MLA_NOPE + MLA_ROPE) ** -0.5
SWA_SCALE = HEAD_DIM ** -0.5
NEG_INF = -1e30

LANES = 128
SUBLANES = 8
MOD_ROWS = 16
VMEM_LIMIT = 56 * 1024 * 1024

C_CONV = 0
C_SGU = 768
C_CQ = 1280
C_KV = 1536
C_SWA = 1792
IN_COLS_P = 2304

BF16 = jnp.bfloat16
F32 = jnp.float32


def _dot(a, b):
    return jnp.dot(a, b, preferred_element_type=F32)


def _dot_nt(a, b):
    return lax.dot_general(a, b, (((1,), (1,)), ((), ())), preferred_element_type=F32)


def _rms(x, g):
    return x * lax.rsqrt(jnp.mean(x * x, axis=-1, keepdims=True) + EPS) * g


def _rope(x, cos, sin_signed, quarter):
    lane = lax.broadcasted_iota(jnp.int32, x.shape, 1)
    first = (lane & quarter) == 0
    rot = jnp.where(first, pltpu.roll(x, LANES - quarter, 1), pltpu.roll(x, quarter, 1))
    return x * cos + rot * sin_signed


def _const_spec(shape):
    nd = len(shape)
    return pl.BlockSpec(shape, lambda *_: (0,) * nd, pipeline_mode=pl.Buffered(1))


def _ada_kernel(c_ref, w_ref, b_ref, o_ref):
    c = c_ref[...]
    s = (c * jax.nn.sigmoid(c)).astype(BF16)
    o_ref[...] = _dot(s, w_ref[...].astype(BF16)) + b_ref[...]


def _ada_call(c_all, w_ada, b_ada):
    depth = w_ada.shape[0]
    tn = 1536
    n_out = N_MOD * D_MODEL
    return pl.pallas_call(
        _ada_kernel,
        out_shape=jax.ShapeDtypeStruct((depth, MOD_ROWS, n_out), F32),
        grid=(depth, n_out // tn),
        in_specs=[
            pl.BlockSpec((MOD_ROWS, D_MODEL), lambda l, j: (0, 0)),
            pl.BlockSpec((None, D_MODEL, tn), lambda l, j: (l, 0, j)),
            pl.BlockSpec((None, 1, tn), lambda l, j: (l, 0, j)),
        ],
        out_specs=pl.BlockSpec((None, MOD_ROWS, tn), lambda l, j: (l, 0, j)),
        compiler_params=pltpu.CompilerParams(
            dimension_semantics=("arbitrary", "arbitrary"), vmem_limit_bytes=VMEM_LIMIT),
        name="ada_mod",
    )(c_all, w_ada, b_ada.reshape(depth, 1, n_out))


def _front_kernel(*refs, rope, cache_out):
    (x_ref, mod_ref, n1_ref, win_ref, sgun_ref, sguw_ref, sgub_ref, qn_ref, wq_ref, kvn_ref,
     wkv_ref) = refs[:11]
    pos = 11
    if rope:
        cos_s_ref, sin_s_ref, cos_m_ref, sin_m_ref = refs[pos:pos + 4]
        pos += 4
    (ab_ref, pc_ref, ysgu_ref, qm_ref, kcat_ref, vm_ref, qs_ref, ks_ref, vs_ref) = refs[pos:pos + 9]
    pos += 9
    if cache_out:
        ckv_o, kpe_o, ksw_o, vsw_o = refs[pos:pos + 4]

    x = x_ref[...]
    t = x.shape[0]
    shift = mod_ref[:, 0:D_MODEL]
    scale = mod_ref[:, D_MODEL:2 * D_MODEL]
    h = (_rms(x, n1_ref[...]) * (1.0 + scale) + shift).astype(BF16)

    zc = _dot(h, win_ref[:, C_CONV:C_CONV + 3 * GROUP_W])
    ab_ref[...] = zc[:, :GROUP_W]
    pc_ref[...] = zc[:, GROUP_W:2 * GROUP_W] * zc[:, 2 * GROUP_W:]

    gz = jax.nn.gelu(_dot(h, win_ref[:, C_SGU:C_SGU + 2 * GROUP_W]), approximate=True)
    u = gz[:, :GROUP_W]
    vn = _rms(gz[:, GROUP_W:], sgun_ref[...])
    head_of_lane = lax.broadcasted_iota(jnp.int32, (CHUNK, GROUP_W), 1) // HEAD_DIM
    for c in range(t // CHUNK):
        rows = slice(c * CHUNK, (c + 1) * CHUNK)
        vc = vn[rows]
        stacked = jnp.concatenate(
            [jnp.where(head_of_lane == hh, vc, 0.0).astype(BF16) for hh in range(SGU_HEADS)], axis=0)
        sv = _dot(sguw_ref[...], stacked) + sgub_ref[...]
        ysgu_ref[rows, :] = (u[rows] * sv).astype(BF16)

    cqn = _rms(_dot(h, win_ref[:, C_CQ:C_CQ + MLA_Q_LORA]), qn_ref[...]).astype(BF16)
    q = _dot(cqn, wq_ref[...]) * MLA_SCALE
    for hh in range(MLA_HEADS):
        qh = q[:, hh * LANES:(hh + 1) * LANES]
        if rope:
            qh = _rope(qh, cos_m_ref[...], sin_m_ref[...], MLA_ROPE // 4)
        qm_ref[:, hh * LANES:(hh + 1) * LANES] = qh.astype(BF16)

    zkv = _dot(h, win_ref[:, C_KV:C_KV + 2 * LANES])
    ckv = _rms(zkv[:, :MLA_KV_LORA], kvn_ref[...])
    kpe = zkv[:, MLA_KV_LORA:]
    if cache_out:
        ckv_o[...] = ckv
        kpe_o[...] = kpe[:, :MLA_ROPE]
    if rope:
        kpe = _rope(kpe, cos_m_ref[...], sin_m_ref[...], MLA_ROPE // 4)
    kv = _dot(ckv.astype(BF16), wkv_ref[...])
    for hh in range(MLA_HEADS):
        kcat_ref[:, hh * LANES:(hh + 1) * LANES] = (kv[:, hh * LANES:(hh + 1) * LANES] + kpe).astype(BF16)
    vm_ref[...] = kv[:, MLA_HEADS * LANES:].astype(BF16)

    zw = _dot(h, win_ref[:, C_SWA:C_SWA + 4 * LANES])
    sk = zw[:, 2 * LANES:3 * LANES]
    sv_ = zw[:, 3 * LANES:]
    if cache_out:
        ksw_o[...] = sk
        vsw_o[...] = sv_
    for g in range(SWA_GROUP):
        qg = zw[:, g * LANES:(g + 1) * LANES] * SWA_SCALE
        if rope:
            qg = _rope(qg, cos_s_ref[...], sin_s_ref[...], HEAD_DIM // 4)
        qs_ref[:, g * LANES:(g + 1) * LANES] = qg.astype(BF16)
    if rope:
        sk = _rope(sk, cos_s_ref[...], sin_s_ref[...], HEAD_DIM // 4)
    ks_ref[...] = sk.astype(BF16)
    vs_ref[...] = sv_.astype(BF16)


def _front_call(x, mod, lw, tables, *, seq, tile, cache_out):
    n = x.shape[0]
    rope = tables is not None
    tiles_per_seq = max(seq // tile, 1)
    n_mod_rows = mod.shape[0]

    def row_spec(width):
        return pl.BlockSpec((tile, width), lambda i: (i, 0))

    def mod_map(i):
        return ((i * tile) // seq if n_mod_rows > 1 else 0, 0, 0)

    in_specs = [
        row_spec(D_MODEL),
        pl.BlockSpec((None, 1, 2 * D_MODEL), mod_map),
        _const_spec((1, D_MODEL)),
        _const_spec((D_MODEL, IN_COLS_P)),
        _const_spec((1, GROUP_W)),
        _const_spec((CHUNK, SGU_HEADS * CHUNK)),
        _const_spec((CHUNK, GROUP_W)),
        _const_spec((1, MLA_Q_LORA)),
        _const_spec((MLA_Q_LORA, MLA_HEADS * LANES)),
        _const_spec((1, MLA_KV_LORA)),
        _const_spec((MLA_KV_LORA, MLA_HEADS * LANES + MLA_HEADS * MLA_V)),
    ]
    args = [x, mod, lw["norm1"], lw["w_in"], lw["sgu_norm"], lw["sgu_w"], lw["sgu_b"], lw["q_norm"],
            lw["w_q_up"], lw["kv_norm"], lw["w_kv_up"]]
    if rope:
        in_specs += [pl.BlockSpec((tile, LANES), lambda i: (i % tiles_per_seq, 0))] * 4
        args += list(tables)

    out_shape = [
        jax.ShapeDtypeStruct((n, GROUP_W), F32),
        jax.ShapeDtypeStruct((n, GROUP_W), F32),
        jax.ShapeDtypeStruct((n, GROUP_W), BF16),
        jax.ShapeDtypeStruct((n, MLA_HEADS * LANES), BF16),
        jax.ShapeDtypeStruct((n, MLA_HEADS * LANES), BF16),
        jax.ShapeDtypeStruct((n, MLA_HEADS * MLA_V), BF16),
        jax.ShapeDtypeStruct((n, 2 * LANES), BF16),
        jax.ShapeDtypeStruct((n, LANES), BF16),
        jax.ShapeDtypeStruct((n, LANES), BF16),
    ]
    if cache_out:
        out_shape += [
            jax.ShapeDtypeStruct((n, MLA_KV_LORA), F32),
            jax.ShapeDtypeStruct((n, MLA_ROPE), F32),
            jax.ShapeDtypeStruct((n, LANES), F32),
            jax.ShapeDtypeStruct((n, LANES), F32),
        ]
    out_specs = [row_spec(s.shape[1]) for s in out_shape]
    return pl.pallas_call(
        functools.partial(_front_kernel, rope=rope, cache_out=cache_out),
        out_shape=out_shape,
        grid=(n // tile,),
        in_specs=in_specs,
        out_specs=out_specs,
        compiler_params=pltpu.CompilerParams(
            dimension_semantics=("arbitrary",), vmem_limit_bytes=VMEM_LIMIT),
        name="front_lat" if rope else "front_ctx",
    )(*args)


def _attend(q, pieces, sink=None):
    scores = []
    for k, _, mask in pieces:
        s = _dot_nt(q, k)
        if mask is not None:
            s = jnp.where(mask, s, NEG_INF)
        scores.append(s)
    m = functools.reduce(jnp.maximum, [jnp.max(s, axis=-1, keepdims=True) for s in scores])
    if sink is not None:
        m = jnp.maximum(m, sink)
    denom = jnp.exp(sink - m) if sink is not None else jnp.zeros_like(m)
    out = None
    for s, (_, v, _) in zip(scores, pieces):
        p = jnp.exp(s - m)
        denom = denom + jnp.sum(p, axis=-1, keepdims=True)
        pv = _dot(p.astype(BF16), v)
        out = pv if out is None else out + pv
    return out * (1.0 / denom)


def _attn_kernel(*refs, latent, tq, seq):
    sink_ref, qm_ref, kcat_ref, vm_ref, qs_ref, ks_ref, vs_ref = refs[:7]
    pos = 7
    if latent:
        cckv_ref, ckpe_ref, cks_ref, cvs_ref, wkv_ref, place_ref = refs[pos:pos + 6]
        pos += 6
    ym_ref, ys_ref = refs[pos:pos + 2]
    pos += 2
    if latent:
        kc_scr, vc_scr, ksc_scr, vsc_scr = refs[pos:pos + 4]
        qi = pl.program_id(1)

        @pl.when(qi == 0)
        def _expand_cache():
            kv = _dot(cckv_ref[...].astype(BF16), wkv_ref[...])
            kp = _dot(ckpe_ref[...].astype(BF16), place_ref[...])
            for hh in range(MLA_HEADS):
                kc_scr[:, hh * LANES:(hh + 1) * LANES] = (kv[:, hh * LANES:(hh + 1) * LANES] + kp).astype(BF16)
            vc_scr[...] = kv[:, MLA_HEADS * LANES:].astype(BF16)
            ksc_scr[...] = cks_ref[...].astype(BF16)
            vsc_scr[...] = cvs_ref[...].astype(BF16)

    head_of_lane = lax.broadcasted_iota(jnp.int32, (tq, MLA_HEADS * MLA_V), 1) // MLA_V
    om = jnp.zeros((tq, MLA_HEADS * MLA_V), F32)
    for hh in range(MLA_HEADS):
        cols = slice(hh * LANES, (hh + 1) * LANES)
        pieces = [(kcat_ref[:, cols], vm_ref[...], None)]
        if latent:
            pieces.append((kc_scr[:, cols], vc_scr[...], None))
        o = _attend(qm_ref[:, cols], pieces)
        om = jnp.where(head_of_lane == hh, o, om)
    ym_ref[...] = om.astype(BF16)

    if latent:
        win = tq + 2 * SWA_WINDOW
        q0 = qi * tq
        start = pl.multiple_of(jnp.clip(q0 - SWA_WINDOW, 0, seq - win), SWA_WINDOW)
        k_band = ks_ref[pl.ds(start, win), :]
        v_band = vs_ref[pl.ds(start, win), :]
        q_pos = q0 + lax.broadcasted_iota(jnp.int32, (tq, win), 0)
        k_pos = start + lax.broadcasted_iota(jnp.int32, (tq, win), 1)
        band_ok = jnp.abs(q_pos - k_pos) <= SWA_WINDOW
        swa_pieces = [(k_band, v_band, band_ok), (ksc_scr[...], vsc_scr[...], None)]
    else:
        swa_pieces = [(ks_ref[...], vs_ref[...], None)]
    kv_of_lane = lax.broadcasted_iota(jnp.int32, (tq, LANES), 1) // HEAD_DIM
    for g in range(SWA_GROUP):
        qg = qs_ref[:, g * LANES:(g + 1) * LANES].astype(F32)
        blk = jnp.zeros((tq, LANES), F32)
        for n in range(SWA_KV_HEADS):
            qh = jnp.where(kv_of_lane == n, qg, 0.0).astype(BF16)
            o = _attend(qh, swa_pieces, sink=sink_ref[n * SWA_GROUP + g])
            blk = jnp.where(kv_of_lane == n, o, blk)
        ys_ref[:, g * LANES:(g + 1) * LANES] = blk.astype(BF16)


def _attn_call(front, sink, lw, cache, *, batch, seq, tq, layer):
    qm, kcat, vm, qs, ks, vs = front
    latent = cache is not None
    n_q = seq // tq

    def q_spec(width):
        return pl.BlockSpec((tq, width), lambda b, i: (b * n_q + i, 0))

    def seq_spec(width):
        return pl.BlockSpec((seq, width), lambda b, i: (b, 0))

    in_specs = [
        pl.BlockSpec(memory_space=pltpu.SMEM),
        q_spec(MLA_HEADS * LANES), seq_spec(MLA_HEADS * LANES), seq_spec(MLA_HEADS * MLA_V),
        q_spec(2 * LANES), seq_spec(LANES), seq_spec(LANES),
    ]
    args = [sink, qm, kcat, vm, qs, ks, vs]
    scratch = []
    if latent:
        c_ckv, c_kpe, c_k, c_v = cache
        past = c_ckv.shape[2]

        def cache_spec(width):
            return pl.BlockSpec((None, None, past, width), lambda b, i: (b, layer, 0, 0))

        in_specs += [cache_spec(MLA_KV_LORA), cache_spec(MLA_ROPE), cache_spec(LANES), cache_spec(LANES),
                     _const_spec((MLA_KV_LORA, MLA_HEADS * LANES + MLA_HEADS * MLA_V)),
                     _const_spec((MLA_ROPE, LANES))]
        args += [c_ckv, c_kpe, c_k, c_v, lw["w_kv_up"], lw["kpe_place"]]
        scratch = [pltpu.VMEM((past, MLA_HEADS * LANES), BF16), pltpu.VMEM((past, MLA_HEADS * MLA_V), BF16),
                   pltpu.VMEM((past, LANES), BF16), pltpu.VMEM((past, LANES), BF16)]
    n = batch * seq
    return pl.pallas_call(
        functools.partial(_attn_kernel, latent=latent, tq=tq, seq=seq),
        out_shape=[jax.ShapeDtypeStruct((n, MLA_HEADS * MLA_V), BF16),
                   jax.ShapeDtypeStruct((n, 2 * LANES), BF16)],
        grid=(batch, n_q),
        in_specs=in_specs,
        out_specs=[q_spec(MLA_HEADS * MLA_V), q_spec(2 * LANES)],
        scratch_shapes=scratch,
        compiler_params=pltpu.CompilerParams(
            dimension_semantics=("arbitrary", "arbitrary"), vmem_limit_bytes=VMEM_LIMIT),
        name="attn_lat" if latent else "attn_ctx",
    )(*args)


def _back_kernel(x_ref, mod_ref, ab_ref, pc_ref, pprev_ref, pnext_ref, ysgu_ref, ym_ref, ys_ref, convw_ref,
                 wout_ref, n2_ref, w1_ref, w2_ref, fn_ref, o_ref, *, seq, final, hidden_chunk):
    t = x_ref.shape[0]
    row = pl.program_id(0) * t + lax.broadcasted_iota(jnp.int32, (t, GROUP_W), 0)
    local = lax.broadcasted_iota(jnp.int32, (t, GROUP_W), 0)
    pc = pc_ref[...]
    prev_row = pprev_ref[SUBLANES - 1:SUBLANES, :]
    next_row = pnext_ref[0:1, :]
    up = jnp.where(local == 0, prev_row, pltpu.roll(pc, 1, 0))
    up = jnp.where(row % seq == 0, 0.0, up)
    dn = jnp.where(local == t - 1, next_row, pltpu.roll(pc, t - 1, 0))
    dn = jnp.where((row + 1) % seq == 0, 0.0, dn)
    conv = convw_ref[0:1, :] * up + convw_ref[1:2, :] * pc + convw_ref[2:3, :] * dn
    y_conv = (ab_ref[...] * conv).astype(BF16)

    y = jnp.concatenate([y_conv, ysgu_ref[...], ym_ref[...], ys_ref[...]], axis=1)
    gate1 = mod_ref[:, 2 * D_MODEL:3 * D_MODEL]
    shift2 = mod_ref[:, 3 * D_MODEL:4 * D_MODEL]
    scale2 = mod_ref[:, 4 * D_MODEL:5 * D_MODEL]
    gate2 = mod_ref[:, 5 * D_MODEL:6 * D_MODEL]
    x1 = x_ref[...] + gate1 * _dot(y, wout_ref[...])
    h2 = (_rms(x1, n2_ref[...]) * (1.0 + scale2) + shift2).astype(BF16)
    acc = jnp.zeros((t, D_MODEL), F32)
    for c in range(MLP_HIDDEN // hidden_chunk):
        cols = slice(c * hidden_chunk, (c + 1) * hidden_chunk)
        hid = jnp.maximum(_dot(h2, w1_ref[:, cols]), 0.0)
        acc = acc + _dot((hid * hid).astype(BF16), w2_ref[cols, :])
    x2 = x1 + gate2 * acc
    if final:
        x2 = _rms(x2, fn_ref[...])
    o_ref[...] = x2


def _back_call(x, mod, ab, pc, ysgu, ym, ys, lw, final_norm, *, seq, tile, final):
    n = x.shape[0]
    n_mod_rows = mod.shape[0]
    halo_blocks = n // SUBLANES
    per_tile = tile // SUBLANES

    def row_spec(width):
        return pl.BlockSpec((tile, width), lambda i: (i, 0))

    def mod_map(i):
        return ((i * tile) // seq if n_mod_rows > 1 else 0, 0, 0)

    in_specs = [
        row_spec(D_MODEL),
        pl.BlockSpec((None, 1, N_MOD * D_MODEL), mod_map),
        row_spec(GROUP_W), row_spec(GROUP_W),
        pl.BlockSpec((SUBLANES, GROUP_W), lambda i: (jnp.maximum(i * per_tile - 1, 0), 0)),
        pl.BlockSpec((SUBLANES, GROUP_W), lambda i: (jnp.minimum((i + 1) * per_tile, halo_blocks - 1), 0)),
        row_spec(GROUP_W), row_spec(GROUP_W), row_spec(GROUP_W),
        _const_spec((CONV_WIDTH, GROUP_W)),
        _const_spec((D_MODEL, D_MODEL)),
        _const_spec((1, D_MODEL)),
        _const_spec((D_MODEL, MLP_HIDDEN)),
        _const_spec((MLP_HIDDEN, D_MODEL)),
        _const_spec((1, D_MODEL)),
    ]
    return pl.pallas_call(
        functools.partial(_back_kernel, seq=seq, final=final, hidden_chunk=1024),
        out_shape=jax.ShapeDtypeStruct((n, D_MODEL), F32),
        grid=(n // tile,),
        in_specs=in_specs,
        out_specs=row_spec(D_MODEL),
        compiler_params=pltpu.CompilerParams(
            dimension_semantics=("arbitrary",), vmem_limit_bytes=VMEM_LIMIT),
        name="back",
    )(x, mod, ab, pc, pc, pc, ysgu, ym, ys, lw["conv_w"], lw["w_out"], lw["norm2"], lw["w1"], lw["w2"],
      final_norm)


def _rope_tables(n_tokens, rot_dim):
    rows = n_tokens // GRID_W
    row = jnp.repeat(jnp.arange(rows, dtype=F32), GRID_W)
    col = jnp.tile(jnp.arange(GRID_W, dtype=F32), rows)
    half = rot_dim // 2
    inv_freq = ROPE_THETA ** (-jnp.arange(0, half, 2, dtype=F32) / half)
    ang_r = row[:, None] * inv_freq[None, :]
    ang_c = col[:, None] * inv_freq[None, :]
    ang = jnp.concatenate([ang_r, ang_r, ang_c, ang_c], axis=-1)
    sign = jnp.where((jnp.arange(rot_dim) & (rot_dim // 4)) == 0, -1.0, 1.0).astype(F32)
    return jnp.cos(ang), jnp.sin(ang) * sign[None, :]


def _lane_tables(n_tokens):
    cos_s, sin_s = _rope_tables(n_tokens, HEAD_DIM)
    cos_s = jnp.tile(cos_s, (1, LANES // HEAD_DIM))
    sin_s = jnp.tile(sin_s, (1, LANES // HEAD_DIM))
    cos_m, sin_m = _rope_tables(n_tokens, MLA_ROPE)
    pad = LANES - MLA_ROPE
    cos_m = jnp.concatenate([cos_m, jnp.ones((n_tokens, pad), F32)], axis=1)
    sin_m = jnp.concatenate([sin_m, jnp.zeros((n_tokens, pad), F32)], axis=1)
    return cos_s, sin_s, cos_m, sin_m


def _layer_weights(l, norm1, norm2, w_in, conv_w, sgu_norm, sgu_w, sgu_b, mla_q_norm, mla_w_q_up, mla_kv_norm,
                   mla_w_kv_up, w_out, mlp_w1, mlp_w2):
    wi = w_in[l]
    d = wi.shape[0]
    c0 = 3 * GROUP_W + 2 * GROUP_W
    cq = wi[:, c0:c0 + MLA_Q_LORA]
    ckv = wi[:, c0 + MLA_Q_LORA:c0 + MLA_Q_LORA + MLA_KV_LORA]
    s_kpe = c0 + MLA_Q_LORA + MLA_KV_LORA
    kpe = wi[:, s_kpe:s_kpe + MLA_ROPE]
    s_q = s_kpe + MLA_ROPE
    sq = wi[:, s_q:s_q + 4 * HEAD_DIM].reshape(d, SWA_KV_HEADS, SWA_GROUP, HEAD_DIM)
    sq = sq.transpose(0, 2, 1, 3).reshape(d, 4 * HEAD_DIM)
    skv = wi[:, s_q + 4 * HEAD_DIM:]
    w_in_p = jnp.concatenate(
        [wi[:, :c0], cq, ckv, kpe, jnp.zeros((d, LANES - MLA_ROPE), F32), sq, skv], axis=1).astype(BF16)

    wq = mla_w_q_up[l].reshape(MLA_Q_LORA, MLA_HEADS, MLA_NOPE + MLA_ROPE)
    wq_p = jnp.concatenate(
        [wq[..., MLA_NOPE:], wq[..., :MLA_NOPE],
         jnp.zeros((MLA_Q_LORA, MLA_HEADS, LANES - MLA_NOPE - MLA_ROPE), F32)], axis=-1)
    wq_p = wq_p.reshape(MLA_Q_LORA, MLA_HEADS * LANES).astype(BF16)

    wkv = mla_w_kv_up[l].reshape(MLA_KV_LORA, MLA_HEADS, MLA_NOPE + MLA_V)
    wk_p = jnp.concatenate(
        [jnp.zeros((MLA_KV_LORA, MLA_HEADS, MLA_ROPE), F32), wkv[..., :MLA_NOPE],
         jnp.zeros((MLA_KV_LORA, MLA_HEADS, LANES - MLA_NOPE - MLA_ROPE), F32)], axis=-1)
    wkv_p = jnp.concatenate(
        [wk_p.reshape(MLA_KV_LORA, MLA_HEADS * LANES), wkv[..., MLA_NOPE:].reshape(MLA_KV_LORA, MLA_HEADS * MLA_V)],
        axis=1).astype(BF16)

    sguw_p = sgu_w[l].transpose(1, 0, 2).reshape(CHUNK, SGU_HEADS * CHUNK).astype(BF16)
    sgub_p = jnp.repeat(sgu_b[l].T, HEAD_DIM, axis=1)

    wo = w_out[l]
    wo_swa = wo[3 * GROUP_W:].reshape(SWA_KV_HEADS, SWA_GROUP, HEAD_DIM, D_MODEL)
    wo_swa = wo_swa.transpose(1, 0, 2, 3).reshape(GROUP_W, D_MODEL)
    wo_p = jnp.concatenate([wo[:3 * GROUP_W], wo_swa], axis=0).astype(BF16)

    place = jnp.eye(MLA_ROPE, LANES, dtype=BF16)
    return {
        "norm1": norm1[l][None, :], "norm2": norm2[l][None, :], "w_in": w_in_p, "conv_w": conv_w[l],
        "sgu_norm": sgu_norm[l][None, :], "sgu_w": sguw_p, "sgu_b": sgub_p,
        "q_norm": mla_q_norm[l][None, :], "w_q_up": wq_p, "kv_norm": mla_kv_norm[l][None, :],
        "w_kv_up": wkv_p, "kpe_place": place, "w_out": wo_p,
        "w1": mlp_w1[l].astype(BF16), "w2": mlp_w2[l].astype(BF16),
    }


def _pick_tile(n, pref):
    while n % pref:
        pref //= 2
    return pref


def kernel(x_prompt, x_sample, cache_mla_ckv, cache_mla_kpe, cache_swa_k, cache_swa_v, c, c_ctx, w_ada, b_ada,
           norm1, norm2, w_in, conv_w, sgu_norm, sgu_w, sgu_b, mla_q_norm, mla_w_q_up, mla_kv_norm, mla_w_kv_up,
           swa_sink, w_out, mlp_w1, mlp_w2, final_norm):
    batch, seq, d = x_prompt.shape
    dec_batch, dec_seq, _ = x_sample.shape
    depth = w_ada.shape[0]
    past = cache_mla_ckv.shape[2]
    assert d == D_MODEL and seq % CHUNK == 0 and dec_seq % (2 * CHUNK) == 0 and 1 + dec_batch <= MOD_ROWS

    c_all = jnp.concatenate(
        [c_ctx[None, :], c, jnp.zeros((MOD_ROWS - 1 - dec_batch, d), F32)], axis=0)
    mod_all = _ada_call(c_all, w_ada, b_ada)

    tables = _lane_tables(dec_seq)
    cache = (cache_mla_ckv, cache_mla_kpe,
             cache_swa_k.reshape(dec_batch, depth, past, LANES), cache_swa_v.reshape(dec_batch, depth, past, LANES))
    fn = final_norm[None, :]

    xp = x_prompt.reshape(batch * seq, d)
    xs = x_sample.reshape(dec_batch * dec_seq, d)
    tile_p = _pick_tile(batch * seq, 512)
    tile_s = _pick_tile(dec_seq, 512)
    tq_s = _pick_tile(dec_seq, 256)
    new_ckv, new_kpe, new_k, new_v = [], [], [], []
    for l in range(depth):
        lw = _layer_weights(l, norm1, norm2, w_in, conv_w, sgu_norm, sgu_w, sgu_b, mla_q_norm, mla_w_q_up,
                            mla_kv_norm, mla_w_kv_up, w_out, mlp_w1, mlp_w2)
        sink = swa_sink[l]
        final = l == depth - 1
        mod_p = mod_all[l, 0:1][None]
        mod_s = mod_all[l, 1:1 + dec_batch][:, None, :]

        fo = _front_call(xp, mod_p, lw, None, seq=seq, tile=tile_p, cache_out=True)
        ab, pc, ysgu = fo[:3]
        new_ckv.append(fo[9].reshape(batch, seq, MLA_KV_LORA))
        new_kpe.append(fo[10].reshape(batch, seq, MLA_ROPE))
        new_k.append(fo[11].reshape(batch, seq, SWA_KV_HEADS, HEAD_DIM))
        new_v.append(fo[12].reshape(batch, seq, SWA_KV_HEADS, HEAD_DIM))
        ym, ys = _attn_call(fo[3:9], sink, lw, None, batch=batch, seq=seq, tq=seq, layer=l)
        xp = _back_call(xp, mod_p, ab, pc, ysgu, ym, ys, lw, fn, seq=seq, tile=tile_p, final=final)

        fo = _front_call(xs, mod_s, lw, tables, seq=dec_seq, tile=tile_s, cache_out=False)
        ab, pc, ysgu = fo[:3]
        ym, ys = _attn_call(fo[3:9], sink, lw, cache, batch=dec_batch, seq=dec_seq, tq=tq_s, layer=l)
        xs = _back_call(xs, mod_s, ab, pc, ysgu, ym, ys, lw, fn, seq=dec_seq, tile=tile_s, final=final)

    return (xp.reshape(batch, seq, d), xs.reshape(dec_batch, dec_seq, d),
            jnp.stack(new_ckv, axis=1), jnp.stack(new_kpe, axis=1),
            jnp.stack(new_k, axis=1), jnp.stack(new_v, axis=1))
```

```python
import functools

import jax
import jax.numpy as jnp
from jax import lax
from jax.experimental import pallas as pl
from jax.experimental.pallas import tpu as pltpu

D_MODEL = 1024
GROUP_W = 256
HEAD_DIM = 64
CONV_WIDTH = 3
CHUNK = 128
SGU_HEADS = 4
MLA_HEADS = 4
MLA_NOPE = 64
MLA_ROPE = 32
MLA_V = 64
MLA_Q_LORA = 256
MLA_KV_LORA = 128
SWA_KV_HEADS = 2
SWA_GROUP = 2
SWA_WINDOW = 128
MLP_HIDDEN = 4096
GRID_W = 64
ROPE_THETA = 10000.0
EPS = 1e-6
N_MOD = 6
MLA_SCALE = (MLA_NOPE + MLA_ROPE) ** -0.5
SWA_SCALE = HEAD_DIM ** -0.5
NEG_INF = -1e30
LOG2E = 1.4426950408889634

LANES = 128
SUBLANES = 8
MOD_ROWS = 16
VMEM_LIMIT = 56 * 1024 * 1024

C_CONV = 0
C_SGU = 768
C_CQ = 1280
C_KV = 1536
C_SWA = 1792
IN_COLS_P = 2304

BF16 = jnp.bfloat16
F32 = jnp.float32


def _dot(a, b):
    return jnp.dot(a, b, preferred_element_type=F32)


def _dot_nt(a, b):
    return lax.dot_general(a, b, (((1,), (1,)), ((), ())), preferred_element_type=F32)


def _rms(x, g):
    return x * lax.rsqrt(jnp.mean(x * x, axis=-1, keepdims=True) + EPS) * g


def _rope(x, cos, sin_signed, quarter):
    lane = lax.broadcasted_iota(jnp.int32, x.shape, 1)
    first = (lane & quarter) == 0
    rot = jnp.where(first, pltpu.roll(x, LANES - quarter, 1), pltpu.roll(x, quarter, 1))
    return x * cos + rot * sin_signed


def _with_ones(v):
    lane = lax.broadcasted_iota(jnp.int32, v.shape, 1)
    ones_half = (((lane >> 6) ^ (lane >> 7)) & 1) == 1
    return jnp.where(ones_half, 1.0, v)


def _const_spec(shape):
    nd = len(shape)
    return pl.BlockSpec(shape, lambda *_: (0,) * nd, pipeline_mode=pl.Buffered(1))


def _ada_kernel(c_ref, w_ref, b_ref, o_ref):
    c = c_ref[...]
    s = (c * jax.nn.sigmoid(c)).astype(BF16)
    o_ref[...] = _dot(s, w_ref[...].astype(BF16)) + b_ref[...]


def _ada_call(c_all, w_ada, b_ada):
    depth = w_ada.shape[0]
    tn = 1536
    n_out = N_MOD * D_MODEL
    return pl.pallas_call(
        _ada_kernel,
        out_shape=jax.ShapeDtypeStruct((depth, MOD_ROWS, n_out), F32),
        grid=(depth, n_out // tn),
        in_specs=[
            pl.BlockSpec((MOD_ROWS, D_MODEL), lambda l, j: (0, 0)),
            pl.BlockSpec((None, D_MODEL, tn), lambda l, j: (l, 0, j)),
            pl.BlockSpec((None, 1, tn), lambda l, j: (l, 0, j)),
        ],
        out_specs=pl.BlockSpec((None, MOD_ROWS, tn), lambda l, j: (l, 0, j)),
        compiler_params=pltpu.CompilerParams(
            dimension_semantics=("arbitrary", "arbitrary"), vmem_limit_bytes=VMEM_LIMIT),
        name="ada_mod",
    )(c_all, w_ada, b_ada.reshape(depth, 1, n_out))


def _front_kernel(*refs, rope, cache_out):
    (x_ref, mod_ref, n1_ref, win_ref, sgun_ref, sguw_ref, sgub_ref, qn_ref, wq_ref, kvn_ref,
     wkv_ref) = refs[:11]
    pos = 11
    if rope:
        cos_s_ref, sin_s_ref, cos_m_ref, sin_m_ref = refs[pos:pos + 4]
        pos += 4
    (ab_ref, pc_ref, ysgu_ref, qm_ref, kcat_ref, vm_ref, qs_ref, ks_ref, vs_ref) = refs[pos:pos + 9]
    pos += 9
    if cache_out:
        ckv_o, kpe_o, ksw_o, vsw_o = refs[pos:pos + 4]

    x = x_ref[...]
    t = x.shape[0]
    shift = mod_ref[:, 0:D_MODEL]
    scale = mod_ref[:, D_MODEL:2 * D_MODEL]
    h = (_rms(x, n1_ref[...]) * (1.0 + scale) + shift).astype(BF16)

    zc = _dot(h, win_ref[:, C_CONV:C_CONV + 3 * GROUP_W])
    ab_ref[...] = zc[:, :GROUP_W]
    pc_ref[...] = zc[:, GROUP_W:2 * GROUP_W] * zc[:, 2 * GROUP_W:]

    gz = jax.nn.gelu(_dot(h, win_ref[:, C_SGU:C_SGU + 2 * GROUP_W]), approximate=True)
    u = gz[:, :GROUP_W]
    vn = _rms(gz[:, GROUP_W:], sgun_ref[...])
    head_of_lane = lax.broadcasted_iota(jnp.int32, (CHUNK, GROUP_W), 1) // HEAD_DIM
    for c in range(t // CHUNK):
        rows = slice(c * CHUNK, (c + 1) * CHUNK)
        vc = vn[rows]
        stacked = jnp.concatenate(
            [jnp.where(head_of_lane == hh, vc, 0.0).astype(BF16) for hh in range(SGU_HEADS)], axis=0)
        sv = _dot(sguw_ref[...], stacked) + sgub_ref[...]
        ysgu_ref[rows, :] = (u[rows] * sv).astype(BF16)

    cqn = _rms(_dot(h, win_ref[:, C_CQ:C_CQ + MLA_Q_LORA]), qn_ref[...]).astype(BF16)
    q = _dot(cqn, wq_ref[...]) * (MLA_SCALE * LOG2E)
    for hh in range(MLA_HEADS):
        qh = q[:, hh * LANES:(hh + 1) * LANES]
        if rope:
            qh = _rope(qh, cos_m_ref[...], sin_m_ref[...], MLA_ROPE // 4)
        qm_ref[:, hh * LANES:(hh + 1) * LANES] = qh.astype(BF16)

    zkv = _dot(h, win_ref[:, C_KV:C_KV + 2 * LANES])
    ckv = _rms(zkv[:, :MLA_KV_LORA], kvn_ref[...])
    kpe = zkv[:, MLA_KV_LORA:]
    if cache_out:
        ckv_o[...] = ckv
        kpe_o[...] = kpe[:, :MLA_ROPE]
    if rope:
        kpe = _rope(kpe, cos_m_ref[...], sin_m_ref[...], MLA_ROPE // 4)
    kv = _dot(ckv.astype(BF16), wkv_ref[...])
    for hh in range(MLA_HEADS):
        kcat_ref[:, hh * LANES:(hh + 1) * LANES] = (kv[:, hh * LANES:(hh + 1) * LANES] + kpe).astype(BF16)
    vm_ref[...] = _with_ones(kv[:, MLA_HEADS * LANES:]).astype(BF16)

    zw = _dot(h, win_ref[:, C_SWA:C_SWA + 4 * LANES])
    sk = zw[:, 2 * LANES:3 * LANES]
    sv_ = zw[:, 3 * LANES:]
    if cache_out:
        ksw_o[...] = sk
        vsw_o[...] = sv_
    for g in range(SWA_GROUP):
        qg = zw[:, g * LANES:(g + 1) * LANES] * (SWA_SCALE * LOG2E)
        if rope:
            qg = _rope(qg, cos_s_ref[...], sin_s_ref[...], HEAD_DIM // 4)
        qs_ref[:, g * LANES:(g + 1) * LANES] = qg.astype(BF16)
    if rope:
        sk = _rope(sk, cos_s_ref[...], sin_s_ref[...], HEAD_DIM // 4)
    ks_ref[...] = sk.astype(BF16)
    vs_ref[...] = _with_ones(jnp.concatenate([sv_, sv_], axis=1)).astype(BF16)


def _front_call(x, mod, lw, tables, *, seq, tile, cache_out):
    n = x.shape[0]
    rope = tables is not None
    tiles_per_seq = max(seq // tile, 1)
    n_mod_rows = mod.shape[0]

    def row_spec(width):
        return pl.BlockSpec((tile, width), lambda i: (i, 0))

    def mod_map(i):
        return ((i * tile) // seq if n_mod_rows > 1 else 0, 0, 0)

    in_specs = [
        row_spec(D_MODEL),
        pl.BlockSpec((None, 1, 2 * D_MODEL), mod_map),
        _const_spec((1, D_MODEL)),
        _const_spec((D_MODEL, IN_COLS_P)),
        _const_spec((1, GROUP_W)),
        _const_spec((CHUNK, SGU_HEADS * CHUNK)),
        _const_spec((CHUNK, GROUP_W)),
        _const_spec((1, MLA_Q_LORA)),
        _const_spec((MLA_Q_LORA, MLA_HEADS * LANES)),
        _const_spec((1, MLA_KV_LORA)),
        _const_spec((MLA_KV_LORA, 2 * MLA_HEADS * LANES)),
    ]
    args = [x, mod, lw["norm1"], lw["w_in"], lw["sgu_norm"], lw["sgu_w"], lw["sgu_b"], lw["q_norm"],
            lw["w_q_up"], lw["kv_norm"], lw["w_kv_up"]]
    if rope:
        in_specs += [pl.BlockSpec((tile, LANES), lambda i: (i % tiles_per_seq, 0))] * 4
        args += list(tables)

    out_shape = [
        jax.ShapeDtypeStruct((n, GROUP_W), F32),
        jax.ShapeDtypeStruct((n, GROUP_W), F32),
        jax.ShapeDtypeStruct((n, GROUP_W), BF16),
        jax.ShapeDtypeStruct((n, MLA_HEADS * LANES), BF16),
        jax.ShapeDtypeStruct((n, MLA_HEADS * LANES), BF16),
        jax.ShapeDtypeStruct((n, MLA_HEADS * LANES), BF16),
        jax.ShapeDtypeStruct((n, 2 * LANES), BF16),
        jax.ShapeDtypeStruct((n, LANES), BF16),
        jax.ShapeDtypeStruct((n, 2 * LANES), BF16),
    ]
    if cache_out:
        out_shape += [
            jax.ShapeDtypeStruct((n, MLA_KV_LORA), F32),
            jax.ShapeDtypeStruct((n, MLA_ROPE), F32),
            jax.ShapeDtypeStruct((n, LANES), F32),
            jax.ShapeDtypeStruct((n, LANES), F32),
        ]
    out_specs = [row_spec(s.shape[1]) for s in out_shape]
    return pl.pallas_call(
        functools.partial(_front_kernel, rope=rope, cache_out=cache_out),
        out_shape=out_shape,
        grid=(n // tile,),
        in_specs=in_specs,
        out_specs=out_specs,
        compiler_params=pltpu.CompilerParams(
            dimension_semantics=("arbitrary",), vmem_limit_bytes=VMEM_LIMIT),
        name="front_lat" if rope else "front_ctx",
    )(*args)


def _attend(q, pieces, sink=None):
    scores = []
    for k, _, mask in pieces:
        s = _dot_nt(q, k)
        if mask is not None:
            s = jnp.where(mask, s, NEG_INF)
        scores.append(s)
    m = functools.reduce(jnp.maximum, [jnp.max(s, axis=-1, keepdims=True) for s in scores])
    if sink is not None:
        m = jnp.maximum(m, sink)
    acc = None
    for s, (_, v, _) in zip(scores, pieces):
        pv = _dot(jnp.exp2(s - m).astype(BF16), v)
        acc = pv if acc is None else acc + pv
    denom = acc if sink is None else acc + jnp.exp2(sink - m)
    return acc * pltpu.roll(1.0 / denom, HEAD_DIM, 1)


def _attn_kernel(*refs, latent, tq, seq):
    sink_ref, qm_ref, kcat_ref, vm_ref, qs_ref, ks_ref, vs_ref = refs[:7]
    pos = 7
    if latent:
        cckv_ref, ckpe_ref, cks_ref, cvs_ref, wkv_ref, place_ref = refs[pos:pos + 6]
        pos += 6
    ym_ref, ys_ref = refs[pos:pos + 2]
    pos += 2
    if latent:
        kc_scr, vc_scr, ksc_scr, vsc_scr = refs[pos:pos + 4]
        qi = pl.program_id(1)

        @pl.when(qi == 0)
        def _expand_cache():
            kv = _dot(cckv_ref[...].astype(BF16), wkv_ref[...])
            kp = _dot(ckpe_ref[...].astype(BF16), place_ref[...])
            for hh in range(MLA_HEADS):
                kc_scr[:, hh * LANES:(hh + 1) * LANES] = (kv[:, hh * LANES:(hh + 1) * LANES] + kp).astype(BF16)
            vc_scr[...] = _with_ones(kv[:, MLA_HEADS * LANES:]).astype(BF16)
            ksc_scr[...] = cks_ref[...].astype(BF16)
            cv = cvs_ref[...]
            vsc_scr[...] = _with_ones(jnp.concatenate([cv, cv], axis=1)).astype(BF16)

    lower = lax.broadcasted_iota(jnp.int32, (tq, LANES), 1) < HEAD_DIM

    for j in range(MLA_HEADS // 2):
        halves = []
        for hh in (2 * j, 2 * j + 1):
            cols = slice(hh * LANES, (hh + 1) * LANES)
            pieces = [(kcat_ref[:, cols], vm_ref[:, cols], None)]
            if latent:
                pieces.append((kc_scr[:, cols], vc_scr[:, cols], None))
            halves.append(_attend(qm_ref[:, cols], pieces))
        ym_ref[:, j * LANES:(j + 1) * LANES] = jnp.where(lower, halves[0], halves[1]).astype(BF16)

    if latent:
        win = tq + 2 * SWA_WINDOW
        q0 = qi * tq
        start = pl.multiple_of(jnp.clip(q0 - SWA_WINDOW, 0, seq - win), SWA_WINDOW)
        q_pos = q0 + lax.broadcasted_iota(jnp.int32, (tq, win), 0)
        k_pos = start + lax.broadcasted_iota(jnp.int32, (tq, win), 1)
        band_ok = jnp.abs(q_pos - k_pos) <= SWA_WINDOW
        k_band = ks_ref[pl.ds(start, win), :]
    for g in range(SWA_GROUP):
        qg = qs_ref[:, g * LANES:(g + 1) * LANES].astype(F32)
        halves = []
        for n in range(SWA_KV_HEADS):
            cols = slice(n * LANES, (n + 1) * LANES)
            if latent:
                pieces = [(k_band, vs_ref[pl.ds(start, win), cols], band_ok),
                          (ksc_scr[...], vsc_scr[:, cols], None)]
            else:
                pieces = [(ks_ref[...], vs_ref[:, cols], None)]
            qh = jnp.where(lower == (n == 0), qg, 0.0).astype(BF16)
            halves.append(_attend(qh, pieces, sink=sink_ref[n * SWA_GROUP + g] * LOG2E))
        ys_ref[:, g * LANES:(g + 1) * LANES] = jnp.where(lower, halves[0], halves[1]).astype(BF16)


def _attn_call(front, sink, lw, cache, *, batch, seq, tq, layer):
    qm, kcat, vm, qs, ks, vs = front
    latent = cache is not None
    n_q = seq // tq

    def q_spec(width):
        return pl.BlockSpec((tq, width), lambda b, i: (b * n_q + i, 0))

    def seq_spec(width):
        return pl.BlockSpec((seq, width), lambda b, i: (b, 0))

    in_specs = [
        pl.BlockSpec(memory_space=pltpu.SMEM),
        q_spec(MLA_HEADS * LANES), seq_spec(MLA_HEADS * LANES), seq_spec(MLA_HEADS * LANES),
        q_spec(2 * LANES), seq_spec(LANES), seq_spec(2 * LANES),
    ]
    args = [sink, qm, kcat, vm, qs, ks, vs]
    scratch = []
    if latent:
        c_ckv, c_kpe, c_k, c_v = cache
        past = c_ckv.shape[2]

        def cache_spec(width):
            return pl.BlockSpec((None, None, past, width), lambda b, i: (b, layer, 0, 0))

        in_specs += [cache_spec(MLA_KV_LORA), cache_spec(MLA_ROPE), cache_spec(LANES), cache_spec(LANES),
                     _const_spec((MLA_KV_LORA, 2 * MLA_HEADS * LANES)),
                     _const_spec((MLA_ROPE, LANES))]
        args += [c_ckv, c_kpe, c_k, c_v, lw["w_kv_up"], lw["kpe_place"]]
        scratch = [pltpu.VMEM((past, MLA_HEADS * LANES), BF16), pltpu.VMEM((past, MLA_HEADS * LANES), BF16),
                   pltpu.VMEM((past, LANES), BF16), pltpu.VMEM((past, 2 * LANES), BF16)]
    n = batch * seq
    return pl.pallas_call(
        functools.partial(_attn_kernel, latent=latent, tq=tq, seq=seq),
        out_shape=[jax.ShapeDtypeStruct((n, MLA_HEADS * MLA_V), BF16),
                   jax.ShapeDtypeStruct((n, 2 * LANES), BF16)],
        grid=(batch, n_q),
        in_specs=in_specs,
        out_specs=[q_spec(MLA_HEADS * MLA_V), q_spec(2 * LANES)],
        scratch_shapes=scratch,
        compiler_params=pltpu.CompilerParams(
            dimension_semantics=("arbitrary", "arbitrary"), vmem_limit_bytes=VMEM_LIMIT),
        name="attn_lat" if latent else "attn_ctx",
    )(*args)


def _back_kernel(x_ref, mod_ref, ab_ref, pc_ref, pprev_ref, pnext_ref, ysgu_ref, ym_ref, ys_ref, convw_ref,
                 wout_ref, n2_ref, w1_ref, w2_ref, fn_ref, o_ref, *, seq, final, hidden_chunk):
    t = x_ref.shape[0]
    row = pl.program_id(0) * t + lax.broadcasted_iota(jnp.int32, (t, GROUP_W), 0)
    local = lax.broadcasted_iota(jnp.int32, (t, GROUP_W), 0)
    pc = pc_ref[...]
    prev_row = pprev_ref[SUBLANES - 1:SUBLANES, :]
    next_row = pnext_ref[0:1, :]
    up = jnp.where(local == 0, prev_row, pltpu.roll(pc, 1, 0))
    up = jnp.where(row % seq == 0, 0.0, up)
    dn = jnp.where(local == t - 1, next_row, pltpu.roll(pc, t - 1, 0))
    dn = jnp.where((row + 1) % seq == 0, 0.0, dn)
    conv = convw_ref[0:1, :] * up + convw_ref[1:2, :] * pc + convw_ref[2:3, :] * dn
    y_conv = (ab_ref[...] * conv).astype(BF16)

    y = jnp.concatenate([y_conv, ysgu_ref[...], ym_ref[...], ys_ref[...]], axis=1)
    gate1 = mod_ref[:, 2 * D_MODEL:3 * D_MODEL]
    shift2 = mod_ref[:, 3 * D_MODEL:4 * D_MODEL]
    scale2 = mod_ref[:, 4 * D_MODEL:5 * D_MODEL]
    gate2 = mod_ref[:, 5 * D_MODEL:6 * D_MODEL]
    x1 = x_ref[...] + gate1 * _dot(y, wout_ref[...])
    h2 = (_rms(x1, n2_ref[...]) * (1.0 + scale2) + shift2).astype(BF16)
    acc = jnp.zeros((t, D_MODEL), F32)
    for c in range(MLP_HIDDEN // hidden_chunk):
        cols = slice(c * hidden_chunk, (c + 1) * hidden_chunk)
        hid = jnp.maximum(_dot(h2, w1_ref[:, cols]), 0.0)
        acc = acc + _dot((hid * hid).astype(BF16), w2_ref[cols, :])
    x2 = x1 + gate2 * acc
    if final:
        x2 = _rms(x2, fn_ref[...])
    o_ref[...] = x2


def _back_call(x, mod, ab, pc, ysgu, ym, ys, lw, final_norm, *, seq, tile, final):
    n = x.shape[0]
    n_mod_rows = mod.shape[0]
    halo_blocks = n // SUBLANES
    per_tile = tile // SUBLANES

    def row_spec(width):
        return pl.BlockSpec((tile, width), lambda i: (i, 0))

    def mod_map(i):
        return ((i * tile) // seq if n_mod_rows > 1 else 0, 0, 0)

    in_specs = [
        row_spec(D_MODEL),
        pl.BlockSpec((None, 1, N_MOD * D_MODEL), mod_map),
        row_spec(GROUP_W), row_spec(GROUP_W),
        pl.BlockSpec((SUBLANES, GROUP_W), lambda i: (jnp.maximum(i * per_tile - 1, 0), 0)),
        pl.BlockSpec((SUBLANES, GROUP_W), lambda i: (jnp.minimum((i + 1) * per_tile, halo_blocks - 1), 0)),
        row_spec(GROUP_W), row_spec(GROUP_W), row_spec(GROUP_W),
        _const_spec((CONV_WIDTH, GROUP_W)),
        _const_spec((D_MODEL, D_MODEL)),
        _const_spec((1, D_MODEL)),
        _const_spec((D_MODEL, MLP_HIDDEN)),
        _const_spec((MLP_HIDDEN, D_MODEL)),
        _const_spec((1, D_MODEL)),
    ]
    return pl.pallas_call(
        functools.partial(_back_kernel, seq=seq, final=final, hidden_chunk=1024),
        out_shape=jax.ShapeDtypeStruct((n, D_MODEL), F32),
        grid=(n // tile,),
        in_specs=in_specs,
        out_specs=row_spec(D_MODEL),
        compiler_params=pltpu.CompilerParams(
            dimension_semantics=("arbitrary",), vmem_limit_bytes=VMEM_LIMIT),
        name="back",
    )(x, mod, ab, pc, pc, pc, ysgu, ym, ys, lw["conv_w"], lw["w_out"], lw["norm2"], lw["w1"], lw["w2"],
      final_norm)


def _rope_tables(n_tokens, rot_dim):
    rows = n_tokens // GRID_W
    row = jnp.repeat(jnp.arange(rows, dtype=F32), GRID_W)
    col = jnp.tile(jnp.arange(GRID_W, dtype=F32), rows)
    half = rot_dim // 2
    inv_freq = ROPE_THETA ** (-jnp.arange(0, half, 2, dtype=F32) / half)
    ang_r = row[:, None] * inv_freq[None, :]
    ang_c = col[:, None] * inv_freq[None, :]
    ang = jnp.concatenate([ang_r, ang_r, ang_c, ang_c], axis=-1)
    sign = jnp.where((jnp.arange(rot_dim) & (rot_dim // 4)) == 0, -1.0, 1.0).astype(F32)
    return jnp.cos(ang), jnp.sin(ang) * sign[None, :]


def _lane_tables(n_tokens):
    cos_s, sin_s = _rope_tables(n_tokens, HEAD_DIM)
    cos_s = jnp.tile(cos_s, (1, LANES // HEAD_DIM))
    sin_s = jnp.tile(sin_s, (1, LANES // HEAD_DIM))
    cos_m, sin_m = _rope_tables(n_tokens, MLA_ROPE)
    pad = LANES - MLA_ROPE
    cos_m = jnp.concatenate([cos_m, jnp.ones((n_tokens, pad), F32)], axis=1)
    sin_m = jnp.concatenate([sin_m, jnp.zeros((n_tokens, pad), F32)], axis=1)
    return cos_s, sin_s, cos_m, sin_m


def _layer_weights(l, norm1, norm2, w_in, conv_w, sgu_norm, sgu_w, sgu_b, mla_q_norm, mla_w_q_up, mla_kv_norm,
                   mla_w_kv_up, w_out, mlp_w1, mlp_w2):
    wi = w_in[l]
    d = wi.shape[0]
    c0 = 3 * GROUP_W + 2 * GROUP_W
    cq = wi[:, c0:c0 + MLA_Q_LORA]
    ckv = wi[:, c0 + MLA_Q_LORA:c0 + MLA_Q_LORA + MLA_KV_LORA]
    s_kpe = c0 + MLA_Q_LORA + MLA_KV_LORA
    kpe = wi[:, s_kpe:s_kpe + MLA_ROPE]
    s_q = s_kpe + MLA_ROPE
    sq = wi[:, s_q:s_q + 4 * HEAD_DIM].reshape(d, SWA_KV_HEADS, SWA_GROUP, HEAD_DIM)
    sq = sq.transpose(0, 2, 1, 3).reshape(d, 4 * HEAD_DIM)
    skv = wi[:, s_q + 4 * HEAD_DIM:]
    w_in_p = jnp.concatenate(
        [wi[:, :c0], cq, ckv, kpe, jnp.zeros((d, LANES - MLA_ROPE), F32), sq, skv], axis=1).astype(BF16)

    wq = mla_w_q_up[l].reshape(MLA_Q_LORA, MLA_HEADS, MLA_NOPE + MLA_ROPE)
    wq_p = jnp.concatenate(
        [wq[..., MLA_NOPE:], wq[..., :MLA_NOPE],
         jnp.zeros((MLA_Q_LORA, MLA_HEADS, LANES - MLA_NOPE - MLA_ROPE), F32)], axis=-1)
    wq_p = wq_p.reshape(MLA_Q_LORA, MLA_HEADS * LANES).astype(BF16)

    wkv = mla_w_kv_up[l].reshape(MLA_KV_LORA, MLA_HEADS, MLA_NOPE + MLA_V)
    wk_p = jnp.concatenate(
        [jnp.zeros((MLA_KV_LORA, MLA_HEADS, MLA_ROPE), F32), wkv[..., :MLA_NOPE],
         jnp.zeros((MLA_KV_LORA, MLA_HEADS, LANES - MLA_NOPE - MLA_ROPE), F32)], axis=-1)
    zeros_v = jnp.zeros((MLA_KV_LORA, MLA_V), F32)
    wv_p = jnp.concatenate(
        [jnp.concatenate([wkv[:, hh, MLA_NOPE:], zeros_v] if hh % 2 == 0 else [zeros_v, wkv[:, hh, MLA_NOPE:]], axis=1)
         for hh in range(MLA_HEADS)], axis=1)
    wkv_p = jnp.concatenate([wk_p.reshape(MLA_KV_LORA, MLA_HEADS * LANES), wv_p], axis=1).astype(BF16)

    sguw_p = sgu_w[l].transpose(1, 0, 2).reshape(CHUNK, SGU_HEADS * CHUNK).astype(BF16)
    sgub_p = jnp.repeat(sgu_b[l].T, HEAD_DIM, axis=1)

    wo = w_out[l]
    wo_swa = wo[3 * GROUP_W:].reshape(SWA_KV_HEADS, SWA_GROUP, HEAD_DIM, D_MODEL)
    wo_swa = wo_swa.transpose(1, 0, 2, 3).reshape(GROUP_W, D_MODEL)
    wo_p = jnp.concatenate([wo[:3 * GROUP_W], wo_swa], axis=0).astype(BF16)

    place = jnp.eye(MLA_ROPE, LANES, dtype=BF16)
    return {
        "norm1": norm1[l][None, :], "norm2": norm2[l][None, :], "w_in": w_in_p, "conv_w": conv_w[l],
        "sgu_norm": sgu_norm[l][None, :], "sgu_w": sguw_p, "sgu_b": sgub_p,
        "q_norm": mla_q_norm[l][None, :], "w_q_up": wq_p, "kv_norm": mla_kv_norm[l][None, :],
        "w_kv_up": wkv_p, "kpe_place": place, "w_out": wo_p,
        "w1": mlp_w1[l].astype(BF16), "w2": mlp_w2[l].astype(BF16),
    }


def _pick_tile(n, pref):
    while n % pref:
        pref //= 2
    return pref


def kernel(x_prompt, x_sample, cache_mla_ckv, cache_mla_kpe, cache_swa_k, cache_swa_v, c, c_ctx, w_ada, b_ada,
           norm1, norm2, w_in, conv_w, sgu_norm, sgu_w, sgu_b, mla_q_norm, mla_w_q_up, mla_kv_norm, mla_w_kv_up,
           swa_sink, w_out, mlp_w1, mlp_w2, final_norm):
    batch, seq, d = x_prompt.shape
    dec_batch, dec_seq, _ = x_sample.shape
    depth = w_ada.shape[0]
    past = cache_mla_ckv.shape[2]
    assert d == D_MODEL and seq % CHUNK == 0 and dec_seq % (2 * CHUNK) == 0 and 1 + dec_batch <= MOD_ROWS

    c_all = jnp.concatenate(
        [c_ctx[None, :], c, jnp.zeros((MOD_ROWS - 1 - dec_batch, d), F32)], axis=0)
    mod_all = _ada_call(c_all, w_ada, b_ada)

    tables = _lane_tables(dec_seq)
    cache = (cache_mla_ckv, cache_mla_kpe,
             cache_swa_k.reshape(dec_batch, depth, past, LANES), cache_swa_v.reshape(dec_batch, depth, past, LANES))
    fn = final_norm[None, :]

    xp = x_prompt.reshape(batch * seq, d)
    xs = x_sample.reshape(dec_batch * dec_seq, d)
    tile_p = _pick_tile(batch * seq, 512)
    tile_s = _pick_tile(dec_seq, 512)
    tq_s = _pick_tile(dec_seq, 256)
    new_ckv, new_kpe, new_k, new_v = [], [], [], []
    for l in range(depth):
        lw = _layer_weights(l, norm1, norm2, w_in, conv_w, sgu_norm, sgu_w, sgu_b, mla_q_norm, mla_w_q_up,
                            mla_kv_norm, mla_w_kv_up, w_out, mlp_w1, mlp_w2)
        sink = swa_sink[l]
        final = l == depth - 1
        mod_p = mod_all[l, 0:1][None]
        mod_s = mod_all[l, 1:1 + dec_batch][:, None, :]

        fo = _front_call(xp, mod_p, lw, None, seq=seq, tile=tile_p, cache_out=True)
        ab, pc, ysgu = fo[:3]
        new_ckv.append(fo[9].reshape(batch, seq, MLA_KV_LORA))
        new_kpe.append(fo[10].reshape(batch, seq, MLA_ROPE))
        new_k.append(fo[11].reshape(batch, seq, SWA_KV_HEADS, HEAD_DIM))
        new_v.append(fo[12].reshape(batch, seq, SWA_KV_HEADS, HEAD_DIM))
        ym, ys = _attn_call(fo[3:9], sink, lw, None, batch=batch, seq=seq, tq=seq, layer=l)
        xp = _back_call(xp, mod_p, ab, pc, ysgu, ym, ys, lw, fn, seq=seq, tile=tile_p, final=final)

        fo = _front_call(xs, mod_s, lw, tables, seq=dec_seq, tile=tile_s, cache_out=False)
        ab, pc, ysgu = fo[:3]
        ym, ys = _attn_call(fo[3:9], sink, lw, cache, batch=dec_batch, seq=dec_seq, tq=tq_s, layer=l)
        xs = _back_call(xs, mod_s, ab, pc, ysgu, ym, ys, lw, fn, seq=dec_seq, tile=tile_s, final=final)

    return (xp.reshape(batch, seq, d), xs.reshape(dec_batch, dec_seq, d),
            jnp.stack(new_ckv, axis=1), jnp.stack(new_kpe, axis=1),
            jnp.stack(new_k, axis=1), jnp.stack(new_v, axis=1))
```

```python
import functools

import jax
import jax.numpy as jnp
from jax import lax
from jax.experimental import pallas as pl
from jax.experimental.pallas import tpu as pltpu

D_MODEL = 1024
GROUP_W = 256
HEAD_DIM = 64
CONV_WIDTH = 3
CHUNK = 128
SGU_HEADS = 4
MLA_HEADS = 4
MLA_NOPE = 64
MLA_ROPE = 32
MLA_V = 64
MLA_Q_LORA = 256
MLA_KV_LORA = 128
SWA_KV_HEADS = 2
SWA_GROUP = 2
SWA_WINDOW = 128
MLP_HIDDEN = 4096
GRID_W = 64
ROPE_THETA = 10000.0
EPS = 1e-6
N_MOD = 6
MLA_SCALE = (MLA_NOPE + MLA_ROPE) ** -0.5
SWA_SCALE = HEAD_DIM ** -0.5
NEG_INF = -1e30
LOG2E = 1.4426950408889634

LANES = 128
SUBLANES = 8
MOD_ROWS = 16
VMEM_LIMIT = 56 * 1024 * 1024

C_CONV = 0
C_SGU = 768
C_CQ = 1280
C_KV = 1536
C_SWA = 1792
IN_COLS_P = 2304
KV_UP_COLS = MLA_HEADS * LANES + MLA_HEADS * MLA_V

BF16 = jnp.bfloat16
F32 = jnp.float32


def _dot(a, b):
    return jnp.dot(a, b, preferred_element_type=F32)


def _dot_nt(a, b):
    return lax.dot_general(a, b, (((1,), (1,)), ((), ())), preferred_element_type=F32)


def _rms(x, g):
    return x * lax.rsqrt(jnp.mean(x * x, axis=-1, keepdims=True) + EPS) * g


def _rope(x, cos, sin_signed, quarter):
    lane = lax.broadcasted_iota(jnp.int32, x.shape, 1)
    first = (lane & quarter) == 0
    rot = jnp.where(first, pltpu.roll(x, LANES - quarter, 1), pltpu.roll(x, quarter, 1))
    return x * cos + rot * sin_signed


def _layer_spec(block, layer):
    nd = len(block)
    return pl.BlockSpec((None,) + tuple(block), lambda *_: (layer,) + (0,) * nd, pipeline_mode=pl.Buffered(1))


def _const_spec(shape):
    nd = len(shape)
    return pl.BlockSpec(shape, lambda *_: (0,) * nd, pipeline_mode=pl.Buffered(1))


def _mod_spec(width, col_block, layer, row_of_step):
    return pl.BlockSpec((None, None, 1, width), lambda *idx: (layer, row_of_step(*idx), 0, col_block))


def _ada_kernel(c_ref, w_ref, b_ref, o_ref):
    c = c_ref[...]
    s = (c * jax.nn.sigmoid(c)).astype(BF16)
    o_ref[...] = _dot(s, w_ref[...].astype(BF16)) + b_ref[...]


def _ada_call(c_all, w_ada, b_ada):
    depth = w_ada.shape[0]
    tn = 1536
    n_out = N_MOD * D_MODEL
    return pl.pallas_call(
        _ada_kernel,
        out_shape=jax.ShapeDtypeStruct((depth, MOD_ROWS, n_out), F32),
        grid=(depth, n_out // tn),
        in_specs=[
            pl.BlockSpec((MOD_ROWS, D_MODEL), lambda l, j: (0, 0)),
            pl.BlockSpec((None, D_MODEL, tn), lambda l, j: (l, 0, j)),
            pl.BlockSpec((None, 1, tn), lambda l, j: (l, 0, j)),
        ],
        out_specs=pl.BlockSpec((None, MOD_ROWS, tn), lambda l, j: (l, 0, j)),
        compiler_params=pltpu.CompilerParams(
            dimension_semantics=("arbitrary", "arbitrary"), vmem_limit_bytes=VMEM_LIMIT),
        name="ada_mod",
    )(c_all, w_ada, b_ada.reshape(depth, 1, n_out))


def _front_kernel(*refs, rope, cache_out):
    (x_ref, mod_ref, n1_ref, win_ref, sgun_ref, sguw_ref, sgub_ref, qn_ref, wq_ref, kvn_ref,
     wkv_ref) = refs[:11]
    pos = 11
    if rope:
        cos_s_ref, sin_s_ref, cos_m_ref, sin_m_ref = refs[pos:pos + 4]
        pos += 4
    (ab_ref, pc_ref, ysgu_ref, qm_ref, kcat_ref, vm_ref, qs_ref, ks_ref, vs_ref) = refs[pos:pos + 9]
    pos += 9
    if cache_out:
        ckv_o, kpe_o, ksw_o, vsw_o = refs[pos:pos + 4]

    shift = mod_ref[:, 0:D_MODEL]
    scale = mod_ref[:, D_MODEL:2 * D_MODEL]
    t = x_ref.shape[0]
    h = (_rms(x_ref[...], n1_ref[...]) * (1.0 + scale) + shift).astype(BF16)

    zq = _dot(h, win_ref[:, C_CQ:C_CQ + MLA_Q_LORA])
    zkv = _dot(h, win_ref[:, C_KV:C_KV + 2 * LANES])
    zs = _dot(h, win_ref[:, C_SGU:C_SGU + 2 * GROUP_W])

    cqn = _rms(zq, qn_ref[...]).astype(BF16)
    q = _dot(cqn, wq_ref[...]) * (MLA_SCALE * LOG2E)
    for hh in range(MLA_HEADS):
        qh = q[:, hh * LANES:(hh + 1) * LANES]
        if rope:
            qh = _rope(qh, cos_m_ref[...], sin_m_ref[...], MLA_ROPE // 4)
        qm_ref[:, hh * LANES:(hh + 1) * LANES] = qh.astype(BF16)

    ckv = _rms(zkv[:, :MLA_KV_LORA], kvn_ref[...])
    kpe = zkv[:, MLA_KV_LORA:]
    if cache_out:
        ckv_o[...] = ckv
        kpe_o[...] = kpe[:, :MLA_ROPE]
    if rope:
        kpe = _rope(kpe, cos_m_ref[...], sin_m_ref[...], MLA_ROPE // 4)
    kv = _dot(ckv.astype(BF16), wkv_ref[...])
    for hh in range(MLA_HEADS):
        kcat_ref[:, hh * LANES:(hh + 1) * LANES] = (kv[:, hh * LANES:(hh + 1) * LANES] + kpe).astype(BF16)
    vm_ref[...] = kv[:, MLA_HEADS * LANES:].astype(BF16)

    zc = _dot(h, win_ref[:, C_CONV:C_CONV + 3 * GROUP_W])
    ab_ref[...] = zc[:, :GROUP_W]
    pc_ref[...] = zc[:, GROUP_W:2 * GROUP_W] * zc[:, 2 * GROUP_W:]

    zw = _dot(h, win_ref[:, C_SWA:C_SWA + 4 * LANES])
    sk = zw[:, 2 * LANES:3 * LANES]
    sv_ = zw[:, 3 * LANES:]
    if cache_out:
        ksw_o[...] = sk
        vsw_o[...] = sv_
    for g in range(SWA_GROUP):
        qg = zw[:, g * LANES:(g + 1) * LANES] * (SWA_SCALE * LOG2E)
        if rope:
            qg = _rope(qg, cos_s_ref[...], sin_s_ref[...], HEAD_DIM // 4)
        qs_ref[:, g * LANES:(g + 1) * LANES] = qg.astype(BF16)
    if rope:
        sk = _rope(sk, cos_s_ref[...], sin_s_ref[...], HEAD_DIM // 4)
    ks_ref[...] = sk.astype(BF16)
    vs_ref[...] = sv_.astype(BF16)

    gz = jax.nn.gelu(zs, approximate=True)
    u = gz[:, :GROUP_W]
    vn = _rms(gz[:, GROUP_W:], sgun_ref[...])
    head_of_lane = lax.broadcasted_iota(jnp.int32, (CHUNK, GROUP_W), 1) // HEAD_DIM
    for c in range(t // CHUNK):
        crow = slice(c * CHUNK, (c + 1) * CHUNK)
        vc = vn[crow]
        stacked = jnp.concatenate(
            [jnp.where(head_of_lane == hh, vc, 0.0).astype(BF16) for hh in range(SGU_HEADS)], axis=0)
        sv = _dot(sguw_ref[...], stacked) + sgub_ref[...]
        ysgu_ref[crow, :] = (u[crow] * sv).astype(BF16)


def _front_call(x, mod, pw, tables, *, layer, seq, tile, mod_row0, cache_out):
    n = x.shape[0]
    rope = tables is not None
    tiles_per_seq = max(seq // tile, 1)

    def row_spec(width):
        return pl.BlockSpec((tile, width), lambda i: (i, 0))

    def mod_row(i):
        return mod_row0 + (i * tile) // seq if rope else mod_row0

    in_specs = [
        row_spec(D_MODEL),
        _mod_spec(2 * D_MODEL, 0, layer, mod_row),
        _layer_spec((1, D_MODEL), layer),
        _layer_spec((D_MODEL, IN_COLS_P), layer),
        _layer_spec((1, GROUP_W), layer),
        _layer_spec((CHUNK, SGU_HEADS * CHUNK), layer),
        _layer_spec((CHUNK, GROUP_W), layer),
        _layer_spec((1, MLA_Q_LORA), layer),
        _layer_spec((MLA_Q_LORA, MLA_HEADS * LANES), layer),
        _layer_spec((1, MLA_KV_LORA), layer),
        _layer_spec((MLA_KV_LORA, KV_UP_COLS), layer),
    ]
    args = [x, mod, pw["norm1"], pw["w_in"], pw["sgu_norm"], pw["sgu_w"], pw["sgu_b"], pw["q_norm"],
            pw["w_q_up"], pw["kv_norm"], pw["w_kv_up"]]
    if rope:
        in_specs += [pl.BlockSpec((tile, LANES), lambda i: (i % tiles_per_seq, 0))] * 4
        args += list(tables)

    out_shape = [
        jax.ShapeDtypeStruct((n, GROUP_W), F32),
        jax.ShapeDtypeStruct((n, GROUP_W), F32),
        jax.ShapeDtypeStruct((n, GROUP_W), BF16),
        jax.ShapeDtypeStruct((n, MLA_HEADS * LANES), BF16),
        jax.ShapeDtypeStruct((n, MLA_HEADS * LANES), BF16),
        jax.ShapeDtypeStruct((n, MLA_HEADS * MLA_V), BF16),
        jax.ShapeDtypeStruct((n, 2 * LANES), BF16),
        jax.ShapeDtypeStruct((n, LANES), BF16),
        jax.ShapeDtypeStruct((n, LANES), BF16),
    ]
    if cache_out:
        out_shape += [
            jax.ShapeDtypeStruct((n, MLA_KV_LORA), F32),
            jax.ShapeDtypeStruct((n, MLA_ROPE), F32),
            jax.ShapeDtypeStruct((n, LANES), F32),
            jax.ShapeDtypeStruct((n, LANES), F32),
        ]
    out_specs = [row_spec(s.shape[1]) for s in out_shape]
    return pl.pallas_call(
        functools.partial(_front_kernel, rope=rope, cache_out=cache_out),
        out_shape=out_shape,
        grid=(n // tile,),
        in_specs=in_specs,
        out_specs=out_specs,
        compiler_params=pltpu.CompilerParams(
            dimension_semantics=("arbitrary",), vmem_limit_bytes=VMEM_LIMIT),
        name="front_lat" if rope else "front_ctx",
    )(*args)


def _scores(q, pieces):
    scores = []
    for k, _, mask in pieces:
        s = _dot_nt(q, k)
        if mask is not None:
            s = jnp.where(mask, s, NEG_INF)
        scores.append(s)
    return scores


def _softmax_pv(scores, pieces, sink):
    m = functools.reduce(jnp.maximum, [jnp.max(s, axis=-1, keepdims=True) for s in scores])
    if sink is not None:
        m = jnp.maximum(m, sink)
    denom = jnp.exp2(sink - m) if sink is not None else jnp.zeros_like(m)
    out = None
    for s, (_, v, _) in zip(scores, pieces):
        p = jnp.exp2(s - m)
        denom = denom + jnp.sum(p, axis=-1, keepdims=True)
        pv = _dot(p.astype(BF16), v)
        out = pv if out is None else out + pv
    return out * (1.0 / denom)


def _attn_kernel(*refs, latent, tq, seq, layer):
    sink_ref, qm_ref, kcat_ref, vm_ref, qs_ref, ks_ref, vs_ref = refs[:7]
    pos = 7
    if latent:
        cckv_ref, ckpe_ref, cks_ref, cvs_ref, wkv_ref, place_ref = refs[pos:pos + 6]
        pos += 6
    ym_ref, ys_ref = refs[pos:pos + 2]
    pos += 2
    if latent:
        kc_scr, vc_scr, ksc_scr, vsc_scr = refs[pos:pos + 4]
        qi = pl.program_id(1)

        @pl.when(qi == 0)
        def _expand_cache():
            kv = _dot(cckv_ref[...].astype(BF16), wkv_ref[...])
            kp = _dot(ckpe_ref[...].astype(BF16), place_ref[...])
            for hh in range(MLA_HEADS):
                kc_scr[:, hh * LANES:(hh + 1) * LANES] = (kv[:, hh * LANES:(hh + 1) * LANES] + kp).astype(BF16)
            vc_scr[...] = kv[:, MLA_HEADS * LANES:].astype(BF16)
            ksc_scr[...] = cks_ref[...].astype(BF16)
            vsc_scr[...] = cvs_ref[...].astype(BF16)

    tasks = []
    for hh in range(MLA_HEADS):
        cols = slice(hh * LANES, (hh + 1) * LANES)
        pieces = [(kcat_ref[:, cols], vm_ref[...], None)]
        if latent:
            pieces.append((kc_scr[:, cols], vc_scr[...], None))
        tasks.append((qm_ref[:, cols], pieces, None))
    if latent:
        win = tq + 2 * SWA_WINDOW
        q0 = qi * tq
        start = pl.multiple_of(jnp.clip(q0 - SWA_WINDOW, 0, seq - win), SWA_WINDOW)
        q_pos = q0 + lax.broadcasted_iota(jnp.int32, (tq, win), 0)
        k_pos = start + lax.broadcasted_iota(jnp.int32, (tq, win), 1)
        band_ok = jnp.abs(q_pos - k_pos) <= SWA_WINDOW
        swa_pieces = [(ks_ref[pl.ds(start, win), :], vs_ref[pl.ds(start, win), :], band_ok),
                      (ksc_scr[...], vsc_scr[...], None)]
    else:
        swa_pieces = [(ks_ref[...], vs_ref[...], None)]
    kv_of_lane = lax.broadcasted_iota(jnp.int32, (tq, LANES), 1) // HEAD_DIM
    for g in range(SWA_GROUP):
        qg = qs_ref[:, g * LANES:(g + 1) * LANES].astype(F32)
        for n in range(SWA_KV_HEADS):
            qh = jnp.where(kv_of_lane == n, qg, 0.0).astype(BF16)
            tasks.append((qh, swa_pieces, sink_ref[layer, n * SWA_GROUP + g] * LOG2E))

    head_of_lane = lax.broadcasted_iota(jnp.int32, (tq, MLA_HEADS * MLA_V), 1) // MLA_V
    om = jnp.zeros((tq, MLA_HEADS * MLA_V), F32)
    blk = jnp.zeros((tq, LANES), F32)
    scores = _scores(tasks[0][0], tasks[0][1])
    for i, (_, pieces, sink) in enumerate(tasks):
        nxt = _scores(tasks[i + 1][0], tasks[i + 1][1]) if i + 1 < len(tasks) else None
        o = _softmax_pv(scores, pieces, sink)
        scores = nxt
        if i < MLA_HEADS:
            om = jnp.where(head_of_lane == i, o, om)
            if i == MLA_HEADS - 1:
                ym_ref[...] = om.astype(BF16)
        else:
            g, n = divmod(i - MLA_HEADS, SWA_KV_HEADS)
            blk = jnp.where(kv_of_lane == n, o, blk)
            if n == SWA_KV_HEADS - 1:
                ys_ref[:, g * LANES:(g + 1) * LANES] = blk.astype(BF16)


def _attn_call(front, sink, pw, cache, *, batch, seq, tq, layer):
    qm, kcat, vm, qs, ks, vs = front
    latent = cache is not None
    n_q = seq // tq

    def q_spec(width):
        return pl.BlockSpec((tq, width), lambda b, i: (b * n_q + i, 0))

    def seq_spec(width):
        return pl.BlockSpec((seq, width), lambda b, i: (b, 0))

    in_specs = [
        pl.BlockSpec(memory_space=pltpu.SMEM),
        q_spec(MLA_HEADS * LANES), seq_spec(MLA_HEADS * LANES), seq_spec(MLA_HEADS * MLA_V),
        q_spec(2 * LANES), seq_spec(LANES), seq_spec(LANES),
    ]
    args = [sink, qm, kcat, vm, qs, ks, vs]
    scratch = []
    if latent:
        c_ckv, c_kpe, c_k, c_v = cache
        past = c_ckv.shape[2]

        def cache_spec(width):
            return pl.BlockSpec((None, None, past, width), lambda b, i: (b, layer, 0, 0))

        in_specs += [cache_spec(MLA_KV_LORA), cache_spec(MLA_ROPE), cache_spec(LANES), cache_spec(LANES),
                     _layer_spec((MLA_KV_LORA, KV_UP_COLS), layer),
                     _const_spec((MLA_ROPE, LANES))]
        args += [c_ckv, c_kpe, c_k, c_v, pw["w_kv_up"], pw["kpe_place"]]
        scratch = [pltpu.VMEM((past, MLA_HEADS * LANES), BF16), pltpu.VMEM((past, MLA_HEADS * MLA_V), BF16),
                   pltpu.VMEM((past, LANES), BF16), pltpu.VMEM((past, LANES), BF16)]
    n = batch * seq
    return pl.pallas_call(
        functools.partial(_attn_kernel, latent=latent, tq=tq, seq=seq, layer=layer),
        out_shape=[jax.ShapeDtypeStruct((n, MLA_HEADS * MLA_V), BF16),
                   jax.ShapeDtypeStruct((n, 2 * LANES), BF16)],
        grid=(batch, n_q),
        in_specs=in_specs,
        out_specs=[q_spec(MLA_HEADS * MLA_V), q_spec(2 * LANES)],
        scratch_shapes=scratch,
        compiler_params=pltpu.CompilerParams(
            dimension_semantics=("arbitrary", "arbitrary"), vmem_limit_bytes=VMEM_LIMIT),
        name="attn_lat" if latent else "attn_ctx",
    )(*args)


def _back_kernel(x_ref, mod_ref, ab_ref, pc_ref, pprev_ref, pnext_ref, ysgu_ref, ym_ref, ys_ref, convw_ref,
                 wout_ref, n2_ref, w1_ref, w2_ref, fn_ref, o_ref, *, seq, final, hidden_chunk):
    t = x_ref.shape[0]
    row = pl.program_id(0) * t + lax.broadcasted_iota(jnp.int32, (t, GROUP_W), 0)
    local = lax.broadcasted_iota(jnp.int32, (t, GROUP_W), 0)
    pc = pc_ref[...]
    prev_row = pprev_ref[SUBLANES - 1:SUBLANES, :]
    next_row = pnext_ref[0:1, :]
    up = jnp.where(local == 0, prev_row, pltpu.roll(pc, 1, 0))
    up = jnp.where(row % seq == 0, 0.0, up)
    dn = jnp.where(local == t - 1, next_row, pltpu.roll(pc, t - 1, 0))
    dn = jnp.where((row + 1) % seq == 0, 0.0, dn)
    conv = convw_ref[0:1, :] * up + convw_ref[1:2, :] * pc + convw_ref[2:3, :] * dn
    y_conv = (ab_ref[...] * conv).astype(BF16)

    y = jnp.concatenate([y_conv, ysgu_ref[...], ym_ref[...], ys_ref[...]], axis=1)
    gate1 = mod_ref[:, 2 * D_MODEL:3 * D_MODEL]
    shift2 = mod_ref[:, 3 * D_MODEL:4 * D_MODEL]
    scale2 = mod_ref[:, 4 * D_MODEL:5 * D_MODEL]
    gate2 = mod_ref[:, 5 * D_MODEL:6 * D_MODEL]
    x1 = x_ref[...] + gate1 * _dot(y, wout_ref[...])
    h2 = (_rms(x1, n2_ref[...]) * (1.0 + scale2) + shift2).astype(BF16)
    acc = jnp.zeros((t, D_MODEL), F32)
    for c in range(MLP_HIDDEN // hidden_chunk):
        cols = slice(c * hidden_chunk, (c + 1) * hidden_chunk)
        hid = jnp.maximum(_dot(h2, w1_ref[:, cols]), 0.0)
        acc = acc + _dot((hid * hid).astype(BF16), w2_ref[cols, :])
    x2 = x1 + gate2 * acc
    if final:
        x2 = _rms(x2, fn_ref[...])
    o_ref[...] = x2


def _back_call(x, mod, ab, pc, ysgu, ym, ys, pw, final_norm, *, layer, seq, tile, mod_row0, per_seq_mod, final):
    n = x.shape[0]
    halo_blocks = n // SUBLANES
    per_tile = tile // SUBLANES

    def row_spec(width):
        return pl.BlockSpec((tile, width), lambda i: (i, 0))

    def mod_row(i):
        return mod_row0 + (i * tile) // seq if per_seq_mod else mod_row0

    in_specs = [
        row_spec(D_MODEL),
        _mod_spec(N_MOD * D_MODEL, 0, layer, mod_row),
        row_spec(GROUP_W), row_spec(GROUP_W),
        pl.BlockSpec((SUBLANES, GROUP_W), lambda i: (jnp.maximum(i * per_tile - 1, 0), 0)),
        pl.BlockSpec((SUBLANES, GROUP_W), lambda i: (jnp.minimum((i + 1) * per_tile, halo_blocks - 1), 0)),
        row_spec(GROUP_W), row_spec(GROUP_W), row_spec(GROUP_W),
        _layer_spec((CONV_WIDTH, GROUP_W), layer),
        _layer_spec((D_MODEL, D_MODEL), layer),
        _layer_spec((1, D_MODEL), layer),
        _layer_spec((D_MODEL, MLP_HIDDEN), layer),
        _layer_spec((MLP_HIDDEN, D_MODEL), layer),
        _const_spec((1, D_MODEL)),
    ]
    return pl.pallas_call(
        functools.partial(_back_kernel, seq=seq, final=final, hidden_chunk=1024),
        out_shape=jax.ShapeDtypeStruct((n, D_MODEL), F32),
        grid=(n // tile,),
        in_specs=in_specs,
        out_specs=row_spec(D_MODEL),
        compiler_params=pltpu.CompilerParams(
            dimension_semantics=("arbitrary",), vmem_limit_bytes=VMEM_LIMIT),
        name="back",
    )(x, mod, ab, pc, pc, pc, ysgu, ym, ys, pw["conv_w"], pw["w_out"], pw["norm2"], pw["w1"], pw["w2"],
      final_norm)


def _rope_tables(n_tokens, rot_dim):
    rows = n_tokens // GRID_W
    row = jnp.repeat(jnp.arange(rows, dtype=F32), GRID_W)
    col = jnp.tile(jnp.arange(GRID_W, dtype=F32), rows)
    half = rot_dim // 2
    inv_freq = ROPE_THETA ** (-jnp.arange(0, half, 2, dtype=F32) / half)
    ang_r = row[:, None] * inv_freq[None, :]
    ang_c = col[:, None] * inv_freq[None, :]
    ang = jnp.concatenate([ang_r, ang_r, ang_c, ang_c], axis=-1)
    sign = jnp.where((jnp.arange(rot_dim) & (rot_dim // 4)) == 0, -1.0, 1.0).astype(F32)
    return jnp.cos(ang), jnp.sin(ang) * sign[None, :]


def _lane_tables(n_tokens):
    cos_s, sin_s = _rope_tables(n_tokens, HEAD_DIM)
    cos_s = jnp.tile(cos_s, (1, LANES // HEAD_DIM))
    sin_s = jnp.tile(sin_s, (1, LANES // HEAD_DIM))
    cos_m, sin_m = _rope_tables(n_tokens, MLA_ROPE)
    pad = LANES - MLA_ROPE
    cos_m = jnp.concatenate([cos_m, jnp.ones((n_tokens, pad), F32)], axis=1)
    sin_m = jnp.concatenate([sin_m, jnp.zeros((n_tokens, pad), F32)], axis=1)
    return cos_s, sin_s, cos_m, sin_m


def _prep_weights(norm1, norm2, w_in, conv_w, sgu_norm, sgu_w, sgu_b, mla_q_norm, mla_w_q_up, mla_kv_norm,
                  mla_w_kv_up, w_out, mlp_w1, mlp_w2):
    depth, d, _ = w_in.shape
    c0 = 3 * GROUP_W + 2 * GROUP_W
    s_kpe = c0 + MLA_Q_LORA + MLA_KV_LORA
    s_q = s_kpe + MLA_ROPE
    sq = w_in[:, :, s_q:s_q + 4 * HEAD_DIM].reshape(depth, d, SWA_KV_HEADS, SWA_GROUP, HEAD_DIM)
    sq = sq.transpose(0, 1, 3, 2, 4).reshape(depth, d, 4 * HEAD_DIM)
    w_in_p = jnp.concatenate(
        [w_in[:, :, :s_kpe], w_in[:, :, s_kpe:s_q], jnp.zeros((depth, d, LANES - MLA_ROPE), F32), sq,
         w_in[:, :, s_q + 4 * HEAD_DIM:]], axis=2).astype(BF16)

    wq = mla_w_q_up.reshape(depth, MLA_Q_LORA, MLA_HEADS, MLA_NOPE + MLA_ROPE)
    wq_p = jnp.concatenate(
        [wq[..., MLA_NOPE:], wq[..., :MLA_NOPE],
         jnp.zeros((depth, MLA_Q_LORA, MLA_HEADS, LANES - MLA_NOPE - MLA_ROPE), F32)], axis=-1)
    wq_p = wq_p.reshape(depth, MLA_Q_LORA, MLA_HEADS * LANES).astype(BF16)

    wkv = mla_w_kv_up.reshape(depth, MLA_KV_LORA, MLA_HEADS, MLA_NOPE + MLA_V)
    wk_p = jnp.concatenate(
        [jnp.zeros((depth, MLA_KV_LORA, MLA_HEADS, MLA_ROPE), F32), wkv[..., :MLA_NOPE],
         jnp.zeros((depth, MLA_KV_LORA, MLA_HEADS, LANES - MLA_NOPE - MLA_ROPE), F32)], axis=-1)
    wkv_p = jnp.concatenate(
        [wk_p.reshape(depth, MLA_KV_LORA, MLA_HEADS * LANES),
         wkv[..., MLA_NOPE:].reshape(depth, MLA_KV_LORA, MLA_HEADS * MLA_V)], axis=2).astype(BF16)

    sguw_p = sgu_w.transpose(0, 2, 1, 3).reshape(depth, CHUNK, SGU_HEADS * CHUNK).astype(BF16)
    sgub_p = jnp.repeat(sgu_b.transpose(0, 2, 1), HEAD_DIM, axis=2)

    wo_swa = w_out[:, 3 * GROUP_W:].reshape(depth, SWA_KV_HEADS, SWA_GROUP, HEAD_DIM, D_MODEL)
    wo_swa = wo_swa.transpose(0, 2, 1, 3, 4).reshape(depth, GROUP_W, D_MODEL)
    wo_p = jnp.concatenate([w_out[:, :3 * GROUP_W], wo_swa], axis=1).astype(BF16)

    return {
        "norm1": norm1[:, None, :], "norm2": norm2[:, None, :], "w_in": w_in_p, "conv_w": conv_w,
        "sgu_norm": sgu_norm[:, None, :], "sgu_w": sguw_p, "sgu_b": sgub_p,
        "q_norm": mla_q_norm[:, None, :], "w_q_up": wq_p, "kv_norm": mla_kv_norm[:, None, :],
        "w_kv_up": wkv_p, "kpe_place": jnp.eye(MLA_ROPE, LANES, dtype=BF16), "w_out": wo_p,
        "w1": mlp_w1.astype(BF16), "w2": mlp_w2.astype(BF16),
    }


def _pick_tile(n, pref):
    while n % pref:
        pref //= 2
    return pref


def kernel(x_prompt, x_sample, cache_mla_ckv, cache_mla_kpe, cache_swa_k, cache_swa_v, c, c_ctx, w_ada, b_ada,
           norm1, norm2, w_in, conv_w, sgu_norm, sgu_w, sgu_b, mla_q_norm, mla_w_q_up, mla_kv_norm, mla_w_kv_up,
           swa_sink, w_out, mlp_w1, mlp_w2, final_norm):
    batch, seq, d = x_prompt.shape
    dec_batch, dec_seq, _ = x_sample.shape
    depth = w_ada.shape[0]
    past = cache_mla_ckv.shape[2]
    assert d == D_MODEL and seq % CHUNK == 0 and dec_seq % (2 * CHUNK) == 0 and 1 + dec_batch <= MOD_ROWS

    c_all = jnp.concatenate(
        [c_ctx[None, :], c, jnp.zeros((MOD_ROWS - 1 - dec_batch, d), F32)], axis=0)
    mod = _ada_call(c_all, w_ada, b_ada).reshape(depth, MOD_ROWS, 1, N_MOD * d)

    pw = _prep_weights(norm1, norm2, w_in, conv_w, sgu_norm, sgu_w, sgu_b, mla_q_norm, mla_w_q_up, mla_kv_norm,
                       mla_w_kv_up, w_out, mlp_w1, mlp_w2)
    tables = _lane_tables(dec_seq)
    cache = (cache_mla_ckv, cache_mla_kpe,
             cache_swa_k.reshape(dec_batch, depth, past, LANES), cache_swa_v.reshape(dec_batch, depth, past, LANES))
    fn = final_norm[None, :]

    xp = x_prompt.reshape(batch * seq, d)
    xs = x_sample.reshape(dec_batch * dec_seq, d)
    tile_p = _pick_tile(batch * seq, 512)
    tile_s = _pick_tile(dec_seq, 512)
    tq_s = _pick_tile(dec_seq, 256)
    new_ckv, new_kpe, new_k, new_v = [], [], [], []
    for l in range(depth):
        final = l == depth - 1

        fo = _front_call(xp, mod, pw, None, layer=l, seq=seq, tile=tile_p, mod_row0=0, cache_out=True)
        ab, pc, ysgu = fo[:3]
        new_ckv.append(fo[9].reshape(batch, seq, MLA_KV_LORA))
        new_kpe.append(fo[10].reshape(batch, seq, MLA_ROPE))
        new_k.append(fo[11].reshape(batch, seq, SWA_KV_HEADS, HEAD_DIM))
        new_v.append(fo[12].reshape(batch, seq, SWA_KV_HEADS, HEAD_DIM))
        ym, ys = _attn_call(fo[3:9], swa_sink, pw, None, batch=batch, seq=seq, tq=seq, layer=l)
        xp = _back_call(xp, mod, ab, pc, ysgu, ym, ys, pw, fn, layer=l, seq=seq, tile=tile_p, mod_row0=0,
                        per_seq_mod=False, final=final)

        fo = _front_call(xs, mod, pw, tables, layer=l, seq=dec_seq, tile=tile_s, mod_row0=1, cache_out=False)
        ab, pc, ysgu = fo[:3]
        ym, ys = _attn_call(fo[3:9], swa_sink, pw, cache, batch=dec_batch, seq=dec_seq, tq=tq_s, layer=l)
        xs = _back_call(xs, mod, ab, pc, ysgu, ym, ys, pw, fn, layer=l, seq=dec_seq, tile=tile_s, mod_row0=1,
                        per_seq_mod=True, final=final)

    return (xp.reshape(batch, seq, d), xs.reshape(dec_batch, dec_seq, d),
            jnp.stack(new_ckv, axis=1), jnp.stack(new_kpe, axis=1),
            jnp.stack(new_k, axis=1), jnp.stack(new_v, axis=1))
```

```python
import functools

import jax
import jax.numpy as jnp
from jax import lax
from jax.experimental import pallas as pl
from jax.experimental.pallas import tpu as pltpu

D_MODEL = 1024
GROUP_W = 256
HEAD_DIM = 64
CONV_WIDTH = 3
CHUNK = 128
SGU_HEADS = 4
MLA_HEADS = 4
MLA_NOPE = 64
MLA_ROPE = 32
MLA_V = 64
MLA_Q_LORA = 256
MLA_KV_LORA = 128
SWA_KV_HEADS = 2
SWA_GROUP = 2
SWA_WINDOW = 128
MLP_HIDDEN = 4096
GRID_W = 64
ROPE_THETA = 10000.0
EPS = 1e-6
N_MOD = 6
MLA_SCALE = (MLA_NOPE + MLA_ROPE) ** -0.5
SWA_SCALE = HEAD_DIM ** -0.5
NEG_INF = -1e30
LOG2E = 1.4426950408889634

LANES = 128
SUBLANES = 8
MOD_ROWS = 16
VMEM_LIMIT = 56 * 1024 * 1024

C_CONV = 0
C_SGU = 768
C_CQ = 1280
IN_MAIN = 1536
T_KV = 0
T_SWA = 256
IN_TAIL = 768
KV_UP_COLS = MLA_HEADS * LANES + MLA_HEADS * MLA_V

BF16 = jnp.bfloat16
F32 = jnp.float32


def _dot(a, b):
    return jnp.dot(a, b, preferred_element_type=F32)


def _dot_nt(a, b):
    return lax.dot_general(a, b, (((1,), (1,)), ((), ())), preferred_element_type=F32)


def _rms(x, g):
    return x * lax.rsqrt(jnp.mean(x * x, axis=-1, keepdims=True) + EPS) * g


def _rope(x, cos, sin_signed, quarter):
    lane = lax.broadcasted_iota(jnp.int32, x.shape, 1)
    first = (lane & quarter) == 0
    rot = jnp.where(first, pltpu.roll(x, LANES - quarter, 1), pltpu.roll(x, quarter, 1))
    return x * cos + rot * sin_signed


def _layer_spec(block, layer):
    nd = len(block)
    return pl.BlockSpec((None,) + tuple(block), lambda *_: (layer,) + (0,) * nd, pipeline_mode=pl.Buffered(1))


def _const_spec(shape):
    nd = len(shape)
    return pl.BlockSpec(shape, lambda *_: (0,) * nd, pipeline_mode=pl.Buffered(1))


def _mod_spec(width, col_block, layer, row_of_step):
    return pl.BlockSpec((None, None, 1, width), lambda *idx: (layer, row_of_step(*idx), 0, col_block))


def _ada_kernel(c_ref, w_ref, b_ref, o_ref):
    c = c_ref[...]
    s = (c * jax.nn.sigmoid(c)).astype(BF16)
    o_ref[...] = _dot(s, w_ref[...].astype(BF16)) + b_ref[...]


def _ada_call(c_all, w_ada, b_ada):
    depth = w_ada.shape[0]
    tn = 1536
    n_out = N_MOD * D_MODEL
    return pl.pallas_call(
        _ada_kernel,
        out_shape=jax.ShapeDtypeStruct((depth, MOD_ROWS, n_out), F32),
        grid=(depth, n_out // tn),
        in_specs=[
            pl.BlockSpec((MOD_ROWS, D_MODEL), lambda l, j: (0, 0)),
            pl.BlockSpec((None, D_MODEL, tn), lambda l, j: (l, 0, j)),
            pl.BlockSpec((None, 1, tn), lambda l, j: (l, 0, j)),
        ],
        out_specs=pl.BlockSpec((None, MOD_ROWS, tn), lambda l, j: (l, 0, j)),
        compiler_params=pltpu.CompilerParams(
            dimension_semantics=("arbitrary", "arbitrary"), vmem_limit_bytes=VMEM_LIMIT),
        name="ada_mod",
    )(c_all, w_ada, b_ada.reshape(depth, 1, n_out))


def _front_kernel(*refs, rope, cache_out):
    (x_ref, mod_ref, n1_ref, win_ref, wtail_ref, sgun_ref, sguw_ref, sgub_ref, qn_ref, wq_ref, kvn_ref,
     wkv_ref) = refs[:12]
    pos = 12
    if rope:
        cos_s_ref, sin_s_ref, cos_m_ref, sin_m_ref = refs[pos:pos + 4]
        pos += 4
    if cache_out:
        pos += 4
    (ab_ref, pc_ref, ysgu_ref, qm_ref, kcat_ref, vm_ref, qs_ref, ks_ref, vs_ref) = refs[pos:pos + 9]
    pos += 9
    if cache_out:
        ckv_o, kpe_o, ksw_o, vsw_o = refs[pos:pos + 4]

        def put(o_ref, val):
            o_ref[...] = val.reshape(o_ref.shape)

    shift = mod_ref[:, 0:D_MODEL]
    scale = mod_ref[:, D_MODEL:2 * D_MODEL]
    t = x_ref.shape[0]
    h = (_rms(x_ref[...], n1_ref[...]) * (1.0 + scale) + shift).astype(BF16)

    zq = _dot(h, win_ref[:, C_CQ:C_CQ + MLA_Q_LORA])
    zkv = _dot(h, wtail_ref[:, T_KV:T_KV + 2 * LANES])
    zs = _dot(h, win_ref[:, C_SGU:C_SGU + 2 * GROUP_W])

    cqn = _rms(zq, qn_ref[...]).astype(BF16)
    q = _dot(cqn, wq_ref[...]) * (MLA_SCALE * LOG2E)
    for hh in range(MLA_HEADS):
        qh = q[:, hh * LANES:(hh + 1) * LANES]
        if rope:
            qh = _rope(qh, cos_m_ref[...], sin_m_ref[...], MLA_ROPE // 4)
        qm_ref[:, hh * LANES:(hh + 1) * LANES] = qh.astype(BF16)

    ckv = _rms(zkv[:, :MLA_KV_LORA], kvn_ref[...])
    kpe = zkv[:, MLA_KV_LORA:]
    if cache_out:
        put(ckv_o, ckv)
        put(kpe_o, kpe[:, :MLA_ROPE])
    if rope:
        kpe = _rope(kpe, cos_m_ref[...], sin_m_ref[...], MLA_ROPE // 4)
    kv = _dot(ckv.astype(BF16), wkv_ref[...])
    for hh in range(MLA_HEADS):
        kcat_ref[:, hh * LANES:(hh + 1) * LANES] = (kv[:, hh * LANES:(hh + 1) * LANES] + kpe).astype(BF16)
    vm_ref[...] = kv[:, MLA_HEADS * LANES:].astype(BF16)

    zc = _dot(h, win_ref[:, C_CONV:C_CONV + 3 * GROUP_W])
    ab_ref[...] = zc[:, :GROUP_W]
    pc_ref[...] = zc[:, GROUP_W:2 * GROUP_W] * zc[:, 2 * GROUP_W:]

    zw = _dot(h, wtail_ref[:, T_SWA:T_SWA + 4 * LANES])
    sk = zw[:, 2 * LANES:3 * LANES]
    sv_ = zw[:, 3 * LANES:]
    if cache_out:
        put(ksw_o, sk)
        put(vsw_o, sv_)
    for g in range(SWA_GROUP):
        qg = zw[:, g * LANES:(g + 1) * LANES] * (SWA_SCALE * LOG2E)
        if rope:
            qg = _rope(qg, cos_s_ref[...], sin_s_ref[...], HEAD_DIM // 4)
        qs_ref[:, g * LANES:(g + 1) * LANES] = qg.astype(BF16)
    if rope:
        sk = _rope(sk, cos_s_ref[...], sin_s_ref[...], HEAD_DIM // 4)
    ks_ref[...] = sk.astype(BF16)
    vs_ref[...] = sv_.astype(BF16)

    gz = jax.nn.gelu(zs, approximate=True)
    u = gz[:, :GROUP_W]
    vn = _rms(gz[:, GROUP_W:], sgun_ref[...])
    head_of_lane = lax.broadcasted_iota(jnp.int32, (CHUNK, GROUP_W), 1) // HEAD_DIM
    for c in range(t // CHUNK):
        crow = slice(c * CHUNK, (c + 1) * CHUNK)
        vc = vn[crow]
        stacked = jnp.concatenate(
            [jnp.where(head_of_lane == hh, vc, 0.0).astype(BF16) for hh in range(SGU_HEADS)], axis=0)
        sv = _dot(sguw_ref[...], stacked) + sgub_ref[...]
        ysgu_ref[crow, :] = (u[crow] * sv).astype(BF16)


def _front_call(x, mod, pw, tables, cache_stacks, *, layer, seq, tile, mod_row0):
    n = x.shape[0]
    rope = tables is not None
    tiles_per_seq = max(seq // tile, 1)

    def row_spec(width):
        return pl.BlockSpec((tile, width), lambda i: (i, 0))

    def mod_row(i):
        return mod_row0 + (i * tile) // seq if rope else mod_row0

    in_specs = [
        row_spec(D_MODEL),
        _mod_spec(2 * D_MODEL, 0, layer, mod_row),
        _layer_spec((1, D_MODEL), layer),
        _layer_spec((D_MODEL, IN_MAIN), layer),
        _layer_spec((D_MODEL, IN_TAIL), layer),
        _layer_spec((1, GROUP_W), layer),
        _layer_spec((CHUNK, SGU_HEADS * CHUNK), layer),
        _layer_spec((CHUNK, GROUP_W), layer),
        _layer_spec((1, MLA_Q_LORA), layer),
        _layer_spec((MLA_Q_LORA, MLA_HEADS * LANES), layer),
        _layer_spec((1, MLA_KV_LORA), layer),
        _layer_spec((MLA_KV_LORA, KV_UP_COLS), layer),
    ]
    args = [x, mod, pw["norm1"], pw["w_in"], pw["w_in_tail"], pw["sgu_norm"], pw["sgu_w"], pw["sgu_b"],
            pw["q_norm"], pw["w_q_up"], pw["kv_norm"], pw["w_kv_up"]]
    if rope:
        in_specs += [pl.BlockSpec((tile, LANES), lambda i: (i % tiles_per_seq, 0))] * 4
        args += list(tables)

    out_shape = [
        jax.ShapeDtypeStruct((n, GROUP_W), F32),
        jax.ShapeDtypeStruct((n, GROUP_W), F32),
        jax.ShapeDtypeStruct((n, GROUP_W), BF16),
        jax.ShapeDtypeStruct((n, MLA_HEADS * LANES), BF16),
        jax.ShapeDtypeStruct((n, MLA_HEADS * LANES), BF16),
        jax.ShapeDtypeStruct((n, MLA_HEADS * MLA_V), BF16),
        jax.ShapeDtypeStruct((n, 2 * LANES), BF16),
        jax.ShapeDtypeStruct((n, LANES), BF16),
        jax.ShapeDtypeStruct((n, LANES), BF16),
    ]
    out_specs = [row_spec(s.shape[1]) for s in out_shape]
    aliases = {}
    if cache_stacks is not None:
        assert tile % seq == 0
        for stack in cache_stacks:
            aliases[len(args)] = len(out_shape)
            in_specs.append(pl.BlockSpec(memory_space=pl.ANY))
            args.append(stack)
            out_shape.append(jax.ShapeDtypeStruct(stack.shape, stack.dtype))
            out_specs.append(pl.BlockSpec((tile // seq, None, seq, stack.shape[3]), lambda i: (i, layer, 0, 0)))
    return pl.pallas_call(
        functools.partial(_front_kernel, rope=rope, cache_out=cache_stacks is not None),
        out_shape=out_shape,
        grid=(n // tile,),
        in_specs=in_specs,
        out_specs=out_specs,
        input_output_aliases=aliases,
        compiler_params=pltpu.CompilerParams(
            dimension_semantics=("arbitrary",), vmem_limit_bytes=VMEM_LIMIT),
        name="front_lat" if rope else "front_ctx",
    )(*args)


def _scores(q, pieces):
    scores = []
    for k, _, mask in pieces:
        s = _dot_nt(q, k)
        if mask is not None:
            s = jnp.where(mask, s, NEG_INF)
        scores.append(s)
    return scores


def _softmax_pv(scores, pieces, sink):
    m = functools.reduce(jnp.maximum, [jnp.max(s, axis=-1, keepdims=True) for s in scores])
    if sink is not None:
        m = jnp.maximum(m, sink)
    denom = jnp.exp2(sink - m) if sink is not None else jnp.zeros_like(m)
    out = None
    for s, (_, v, _) in zip(scores, pieces):
        p = jnp.exp2(s - m)
        denom = denom + jnp.sum(p, axis=-1, keepdims=True)
        pv = _dot(p.astype(BF16), v)
        out = pv if out is None else out + pv
    return out * (1.0 / denom)


def _attn_kernel(*refs, latent, tq, seq, layer):
    sink_ref, qm_ref, kcat_ref, vm_ref, qs_ref, ks_ref, vs_ref = refs[:7]
    pos = 7
    if latent:
        cckv_ref, ckpe_ref, cks_ref, cvs_ref, wkv_ref, place_ref = refs[pos:pos + 6]
        pos += 6
    ym_ref, ys_ref = refs[pos:pos + 2]
    pos += 2
    if latent:
        kc_scr, vc_scr, ksc_scr, vsc_scr = refs[pos:pos + 4]
        qi = pl.program_id(1)

        @pl.when(qi == 0)
        def _expand_cache():
            kv = _dot(cckv_ref[...].astype(BF16), wkv_ref[...])
            kp = _dot(ckpe_ref[...].astype(BF16), place_ref[...])
            for hh in range(MLA_HEADS):
                kc_scr[:, hh * LANES:(hh + 1) * LANES] = (kv[:, hh * LANES:(hh + 1) * LANES] + kp).astype(BF16)
            vc_scr[...] = kv[:, MLA_HEADS * LANES:].astype(BF16)
            ksc_scr[...] = cks_ref[...].astype(BF16)
            vsc_scr[...] = cvs_ref[...].astype(BF16)

    tasks = []
    for hh in range(MLA_HEADS):
        cols = slice(hh * LANES, (hh + 1) * LANES)
        pieces = [(kcat_ref[:, cols], vm_ref[...], None)]
        if latent:
            pieces.append((kc_scr[:, cols], vc_scr[...], None))
        tasks.append((qm_ref[:, cols], pieces, None))
    if latent:
        win = tq + 2 * SWA_WINDOW
        q0 = qi * tq
        start = pl.multiple_of(jnp.clip(q0 - SWA_WINDOW, 0, seq - win), SWA_WINDOW)
        q_pos = q0 + lax.broadcasted_iota(jnp.int32, (tq, win), 0)
        k_pos = start + lax.broadcasted_iota(jnp.int32, (tq, win), 1)
        band_ok = jnp.abs(q_pos - k_pos) <= SWA_WINDOW
        swa_pieces = [(ks_ref[pl.ds(start, win), :], vs_ref[pl.ds(start, win), :], band_ok),
                      (ksc_scr[...], vsc_scr[...], None)]
    else:
        swa_pieces = [(ks_ref[...], vs_ref[...], None)]
    kv_of_lane = lax.broadcasted_iota(jnp.int32, (tq, LANES), 1) // HEAD_DIM
    for g in range(SWA_GROUP):
        qg = qs_ref[:, g * LANES:(g + 1) * LANES].astype(F32)
        for n in range(SWA_KV_HEADS):
            qh = jnp.where(kv_of_lane == n, qg, 0.0).astype(BF16)
            tasks.append((qh, swa_pieces, sink_ref[layer, n * SWA_GROUP + g] * LOG2E))

    head_of_lane = lax.broadcasted_iota(jnp.int32, (tq, MLA_HEADS * MLA_V), 1) // MLA_V
    om = jnp.zeros((tq, MLA_HEADS * MLA_V), F32)
    blk = jnp.zeros((tq, LANES), F32)
    scores = _scores(tasks[0][0], tasks[0][1])
    for i, (_, pieces, sink) in enumerate(tasks):
        nxt = _scores(tasks[i + 1][0], tasks[i + 1][1]) if i + 1 < len(tasks) else None
        o = _softmax_pv(scores, pieces, sink)
        scores = nxt
        if i < MLA_HEADS:
            om = jnp.where(head_of_lane == i, o, om)
            if i == MLA_HEADS - 1:
                ym_ref[...] = om.astype(BF16)
        else:
            g, n = divmod(i - MLA_HEADS, SWA_KV_HEADS)
            blk = jnp.where(kv_of_lane == n, o, blk)
            if n == SWA_KV_HEADS - 1:
                ys_ref[:, g * LANES:(g + 1) * LANES] = blk.astype(BF16)


def _attn_call(front, sink, pw, cache, *, batch, seq, tq, layer):
    qm, kcat, vm, qs, ks, vs = front
    latent = cache is not None
    n_q = seq // tq

    def q_spec(width):
        return pl.BlockSpec((tq, width), lambda b, i: (b * n_q + i, 0))

    def seq_spec(width):
        return pl.BlockSpec((seq, width), lambda b, i: (b, 0))

    in_specs = [
        pl.BlockSpec(memory_space=pltpu.SMEM),
        q_spec(MLA_HEADS * LANES), seq_spec(MLA_HEADS * LANES), seq_spec(MLA_HEADS * MLA_V),
        q_spec(2 * LANES), seq_spec(LANES), seq_spec(LANES),
    ]
    args = [sink, qm, kcat, vm, qs, ks, vs]
    scratch = []
    if latent:
        c_ckv, c_kpe, c_k, c_v = cache
        past = c_ckv.shape[2]

        def cache_spec(width):
            return pl.BlockSpec((None, None, past, width), lambda b, i: (b, layer, 0, 0))

        in_specs += [cache_spec(MLA_KV_LORA), cache_spec(MLA_ROPE), cache_spec(LANES), cache_spec(LANES),
                     _layer_spec((MLA_KV_LORA, KV_UP_COLS), layer),
                     _const_spec((MLA_ROPE, LANES))]
        args += [c_ckv, c_kpe, c_k, c_v, pw["w_kv_up"], pw["kpe_place"]]
        scratch = [pltpu.VMEM((past, MLA_HEADS * LANES), BF16), pltpu.VMEM((past, MLA_HEADS * MLA_V), BF16),
                   pltpu.VMEM((past, LANES), BF16), pltpu.VMEM((past, LANES), BF16)]
    n = batch * seq
    return pl.pallas_call(
        functools.partial(_attn_kernel, latent=latent, tq=tq, seq=seq, layer=layer),
        out_shape=[jax.ShapeDtypeStruct((n, MLA_HEADS * MLA_V), BF16),
                   jax.ShapeDtypeStruct((n, 2 * LANES), BF16)],
        grid=(batch, n_q),
        in_specs=in_specs,
        out_specs=[q_spec(MLA_HEADS * MLA_V), q_spec(2 * LANES)],
        scratch_shapes=scratch,
        compiler_params=pltpu.CompilerParams(
            dimension_semantics=("arbitrary", "arbitrary"), vmem_limit_bytes=VMEM_LIMIT),
        name="attn_lat" if latent else "attn_ctx",
    )(*args)


def _back_kernel(x_ref, mod_ref, ab_ref, pc_ref, pprev_ref, pnext_ref, ysgu_ref, ym_ref, ys_ref, convw_ref,
                 wout_ref, n2_ref, w1_ref, w2_ref, fn_ref, o_ref, *, seq, final, hidden_chunk):
    t = x_ref.shape[0]
    row = pl.program_id(0) * t + lax.broadcasted_iota(jnp.int32, (t, GROUP_W), 0)
    local = lax.broadcasted_iota(jnp.int32, (t, GROUP_W), 0)
    pc = pc_ref[...]
    prev_row = pprev_ref[SUBLANES - 1:SUBLANES, :]
    next_row = pnext_ref[0:1, :]
    up = jnp.where(local == 0, prev_row, pltpu.roll(pc, 1, 0))
    up = jnp.where(row % seq == 0, 0.0, up)
    dn = jnp.where(local == t - 1, next_row, pltpu.roll(pc, t - 1, 0))
    dn = jnp.where((row + 1) % seq == 0, 0.0, dn)
    conv = convw_ref[0:1, :] * up + convw_ref[1:2, :] * pc + convw_ref[2:3, :] * dn
    y_conv = (ab_ref[...] * conv).astype(BF16)

    y = jnp.concatenate([y_conv, ysgu_ref[...], ym_ref[...], ys_ref[...]], axis=1)
    gate1 = mod_ref[:, 2 * D_MODEL:3 * D_MODEL]
    shift2 = mod_ref[:, 3 * D_MODEL:4 * D_MODEL]
    scale2 = mod_ref[:, 4 * D_MODEL:5 * D_MODEL]
    gate2 = mod_ref[:, 5 * D_MODEL:6 * D_MODEL]
    x1 = x_ref[...] + gate1 * _dot(y, wout_ref[...])
    h2 = (_rms(x1, n2_ref[...]) * (1.0 + scale2) + shift2).astype(BF16)
    acc = jnp.zeros((t, D_MODEL), F32)
    for c in range(MLP_HIDDEN // hidden_chunk):
        cols = slice(c * hidden_chunk, (c + 1) * hidden_chunk)
        hid = jnp.maximum(_dot(h2, w1_ref[:, cols]), 0.0)
        acc = acc + _dot((hid * hid).astype(BF16), w2_ref[cols, :])
    x2 = x1 + gate2 * acc
    if final:
        x2 = _rms(x2, fn_ref[...])
    o_ref[...] = x2


def _back_call(x, mod, ab, pc, ysgu, ym, ys, pw, final_norm, *, layer, seq, tile, mod_row0, per_seq_mod, final):
    n = x.shape[0]
    halo_blocks = n // SUBLANES
    per_tile = tile // SUBLANES

    def row_spec(width):
        return pl.BlockSpec((tile, width), lambda i: (i, 0))

    def mod_row(i):
        return mod_row0 + (i * tile) // seq if per_seq_mod else mod_row0

    in_specs = [
        row_spec(D_MODEL),
        _mod_spec(N_MOD * D_MODEL, 0, layer, mod_row),
        row_spec(GROUP_W), row_spec(GROUP_W),
        pl.BlockSpec((SUBLANES, GROUP_W), lambda i: (jnp.maximum(i * per_tile - 1, 0), 0)),
        pl.BlockSpec((SUBLANES, GROUP_W), lambda i: (jnp.minimum((i + 1) * per_tile, halo_blocks - 1), 0)),
        row_spec(GROUP_W), row_spec(GROUP_W), row_spec(GROUP_W),
        _layer_spec((CONV_WIDTH, GROUP_W), layer),
        _layer_spec((D_MODEL, D_MODEL), layer),
        _layer_spec((1, D_MODEL), layer),
        _layer_spec((D_MODEL, MLP_HIDDEN), layer),
        _layer_spec((MLP_HIDDEN, D_MODEL), layer),
        _const_spec((1, D_MODEL)),
    ]
    return pl.pallas_call(
        functools.partial(_back_kernel, seq=seq, final=final, hidden_chunk=1024),
        out_shape=jax.ShapeDtypeStruct((n, D_MODEL), F32),
        grid=(n // tile,),
        in_specs=in_specs,
        out_specs=row_spec(D_MODEL),
        compiler_params=pltpu.CompilerParams(
            dimension_semantics=("arbitrary",), vmem_limit_bytes=VMEM_LIMIT),
        name="back",
    )(x, mod, ab, pc, pc, pc, ysgu, ym, ys, pw["conv_w"], pw["w_out"], pw["norm2"], pw["w1"], pw["w2"],
      final_norm)


def _rope_tables(n_tokens, rot_dim):
    rows = n_tokens // GRID_W
    row = jnp.repeat(jnp.arange(rows, dtype=F32), GRID_W)
    col = jnp.tile(jnp.arange(GRID_W, dtype=F32), rows)
    half = rot_dim // 2
    inv_freq = ROPE_THETA ** (-jnp.arange(0, half, 2, dtype=F32) / half)
    ang_r = row[:, None] * inv_freq[None, :]
    ang_c = col[:, None] * inv_freq[None, :]
    ang = jnp.concatenate([ang_r, ang_r, ang_c, ang_c], axis=-1)
    sign = jnp.where((jnp.arange(rot_dim) & (rot_dim // 4)) == 0, -1.0, 1.0).astype(F32)
    return jnp.cos(ang), jnp.sin(ang) * sign[None, :]


def _lane_tables(n_tokens):
    cos_s, sin_s = _rope_tables(n_tokens, HEAD_DIM)
    cos_s = jnp.tile(cos_s, (1, LANES // HEAD_DIM))
    sin_s = jnp.tile(sin_s, (1, LANES // HEAD_DIM))
    cos_m, sin_m = _rope_tables(n_tokens, MLA_ROPE)
    pad = LANES - MLA_ROPE
    cos_m = jnp.concatenate([cos_m, jnp.ones((n_tokens, pad), F32)], axis=1)
    sin_m = jnp.concatenate([sin_m, jnp.zeros((n_tokens, pad), F32)], axis=1)
    return cos_s, sin_s, cos_m, sin_m


def _prep_weights(norm1, norm2, w_in, conv_w, sgu_norm, sgu_w, sgu_b, mla_q_norm, mla_w_q_up, mla_kv_norm,
                  mla_w_kv_up, w_out, mlp_w1, mlp_w2):
    depth, d, _ = w_in.shape
    s_kpe = IN_MAIN + MLA_KV_LORA
    s_q = s_kpe + MLA_ROPE
    w_in_main = w_in[:, :, :IN_MAIN].astype(BF16)
    sq = w_in[:, :, s_q:s_q + 4 * HEAD_DIM].reshape(depth, d, SWA_KV_HEADS, SWA_GROUP, HEAD_DIM)
    sq = sq.transpose(0, 1, 3, 2, 4).reshape(depth, d, 4 * HEAD_DIM)
    w_in_tail = jnp.concatenate(
        [w_in[:, :, IN_MAIN:s_q], jnp.zeros((depth, d, LANES - MLA_ROPE), F32), sq,
         w_in[:, :, s_q + 4 * HEAD_DIM:]], axis=2).astype(BF16)

    wq = mla_w_q_up.reshape(depth, MLA_Q_LORA, MLA_HEADS, MLA_NOPE + MLA_ROPE)
    wq_p = jnp.concatenate(
        [wq[..., MLA_NOPE:], wq[..., :MLA_NOPE],
         jnp.zeros((depth, MLA_Q_LORA, MLA_HEADS, LANES - MLA_NOPE - MLA_ROPE), F32)], axis=-1)
    wq_p = wq_p.reshape(depth, MLA_Q_LORA, MLA_HEADS * LANES).astype(BF16)

    wkv = mla_w_kv_up.reshape(depth, MLA_KV_LORA, MLA_HEADS, MLA_NOPE + MLA_V)
    wk_p = jnp.concatenate(
        [jnp.zeros((depth, MLA_KV_LORA, MLA_HEADS, MLA_ROPE), F32), wkv[..., :MLA_NOPE],
         jnp.zeros((depth, MLA_KV_LORA, MLA_HEADS, LANES - MLA_NOPE - MLA_ROPE), F32)], axis=-1)
    wkv_p = jnp.concatenate(
        [wk_p.reshape(depth, MLA_KV_LORA, MLA_HEADS * LANES),
         wkv[..., MLA_NOPE:].reshape(depth, MLA_KV_LORA, MLA_HEADS * MLA_V)], axis=2).astype(BF16)

    sguw_p = sgu_w.transpose(0, 2, 1, 3).reshape(depth, CHUNK, SGU_HEADS * CHUNK).astype(BF16)
    sgub_p = jnp.repeat(sgu_b.transpose(0, 2, 1), HEAD_DIM, axis=2)

    wo_swa = w_out[:, 3 * GROUP_W:].reshape(depth, SWA_KV_HEADS, SWA_GROUP, HEAD_DIM, D_MODEL)
    wo_swa = wo_swa.transpose(0, 2, 1, 3, 4).reshape(depth, GROUP_W, D_MODEL)
    wo_p = jnp.concatenate([w_out[:, :3 * GROUP_W], wo_swa], axis=1).astype(BF16)

    return {
        "norm1": norm1[:, None, :], "norm2": norm2[:, None, :], "w_in": w_in_main, "w_in_tail": w_in_tail,
        "conv_w": conv_w,
        "sgu_norm": sgu_norm[:, None, :], "sgu_w": sguw_p, "sgu_b": sgub_p,
        "q_norm": mla_q_norm[:, None, :], "w_q_up": wq_p, "kv_norm": mla_kv_norm[:, None, :],
        "w_kv_up": wkv_p, "kpe_place": jnp.eye(MLA_ROPE, LANES, dtype=BF16), "w_out": wo_p,
        "w1": mlp_w1.astype(BF16), "w2": mlp_w2.astype(BF16),
    }


def _pick_tile(n, pref):
    while n % pref:
        pref //= 2
    return pref


def kernel(x_prompt, x_sample, cache_mla_ckv, cache_mla_kpe, cache_swa_k, cache_swa_v, c, c_ctx, w_ada, b_ada,
           norm1, norm2, w_in, conv_w, sgu_norm, sgu_w, sgu_b, mla_q_norm, mla_w_q_up, mla_kv_norm, mla_w_kv_up,
           swa_sink, w_out, mlp_w1, mlp_w2, final_norm):
    batch, seq, d = x_prompt.shape
    dec_batch, dec_seq, _ = x_sample.shape
    depth = w_ada.shape[0]
    past = cache_mla_ckv.shape[2]
    assert d == D_MODEL and seq % CHUNK == 0 and dec_seq % (2 * CHUNK) == 0 and 1 + dec_batch <= MOD_ROWS

    c_all = jnp.concatenate(
        [c_ctx[None, :], c, jnp.zeros((MOD_ROWS - 1 - dec_batch, d), F32)], axis=0)
    mod = _ada_call(c_all, w_ada, b_ada).reshape(depth, MOD_ROWS, 1, N_MOD * d)

    pw = _prep_weights(norm1, norm2, w_in, conv_w, sgu_norm, sgu_w, sgu_b, mla_q_norm, mla_w_q_up, mla_kv_norm,
                       mla_w_kv_up, w_out, mlp_w1, mlp_w2)
    tables = _lane_tables(dec_seq)
    cache = (cache_mla_ckv, cache_mla_kpe,
             cache_swa_k.reshape(dec_batch, depth, past, LANES), cache_swa_v.reshape(dec_batch, depth, past, LANES))
    fn = final_norm[None, :]

    xp = x_prompt.reshape(batch * seq, d)
    xs = x_sample.reshape(dec_batch * dec_seq, d)
    tile_p = _pick_tile(batch * seq, 512)
    tile_s = _pick_tile(dec_seq, 512)
    tq_s = _pick_tile(dec_seq, 256)
    new_cache = [jnp.zeros((batch, depth, seq, w), F32) for w in (MLA_KV_LORA, MLA_ROPE, LANES, LANES)]
    for l in range(depth):
        final = l == depth - 1

        fo = _front_call(xp, mod, pw, None, new_cache, layer=l, seq=seq, tile=tile_p, mod_row0=0)
        ab, pc, ysgu = fo[:3]
        new_cache = list(fo[9:13])
        ym, ys = _attn_call(fo[3:9], swa_sink, pw, None, batch=batch, seq=seq, tq=seq, layer=l)
        xp = _back_call(xp, mod, ab, pc, ysgu, ym, ys, pw, fn, layer=l, seq=seq, tile=tile_p, mod_row0=0,
                        per_seq_mod=False, final=final)

        fo = _front_call(xs, mod, pw, tables, None, layer=l, seq=dec_seq, tile=tile_s, mod_row0=1)
        ab, pc, ysgu = fo[:3]
        ym, ys = _attn_call(fo[3:9], swa_sink, pw, cache, batch=dec_batch, seq=dec_seq, tq=tq_s, layer=l)
        xs = _back_call(xs, mod, ab, pc, ysgu, ym, ys, pw, fn, layer=l, seq=dec_seq, tile=tile_s, mod_row0=1,
                        per_seq_mod=True, final=final)

    swa_shape = (batch, depth, seq, SWA_KV_HEADS, HEAD_DIM)
    return (xp.reshape(batch, seq, d), xs.reshape(dec_batch, dec_seq, d), new_cache[0], new_cache[1],
            new_cache[2].reshape(swa_shape), new_cache[3].reshape(swa_shape))
```

```python
import functools

import jax
import jax.numpy as jnp
from jax import lax
from jax.experimental import pallas as pl
from jax.experimental.pallas import tpu as pltpu

D_MODEL = 1024
GROUP_W = 256
HEAD_DIM = 64
CONV_WIDTH = 3
CHUNK = 128
SGU_HEADS = 4
MLA_HEADS = 4
MLA_NOPE = 64
MLA_ROPE = 32
MLA_V = 64
MLA_Q_LORA = 256
MLA_KV_LORA = 128
SWA_KV_HEADS = 2
SWA_GROUP = 2
SWA_WINDOW = 128
MLP_HIDDEN = 4096
GRID_W = 64
ROPE_THETA = 10000.0
EPS = 1e-6
N_MOD = 6
MLA_SCALE = (MLA_NOPE + MLA_ROPE) ** -0.5
SWA_SCALE = HEAD_DIM ** -0.5
NEG_INF = -1e30
LOG2E = 1.4426950408889634

LANES = 128
SUBLANES = 8
MOD_ROWS = 16
VMEM_LIMIT = 56 * 1024 * 1024

C_CONV = 0
C_SGU = 768
C_CQ = 1280
IN_MAIN = 1536
T_KV = 0
T_SWA = 256
IN_TAIL = 768
KV_UP_COLS = MLA_HEADS * LANES + MLA_HEADS * MLA_V

BF16 = jnp.bfloat16
F32 = jnp.float32


def _dot(a, b):
    return jnp.dot(a, b, preferred_element_type=F32)


def _dot_nt(a, b):
    return lax.dot_general(a, b, (((1,), (1,)), ((), ())), preferred_element_type=F32)


def _rms(x, g):
    return x * lax.rsqrt(jnp.mean(x * x, axis=-1, keepdims=True) + EPS) * g


def _rope(x, cos, sin_signed, quarter):
    lane = lax.broadcasted_iota(jnp.int32, x.shape, 1)
    first = (lane & quarter) == 0
    rot = jnp.where(first, pltpu.roll(x, LANES - quarter, 1), pltpu.roll(x, quarter, 1))
    return x * cos + rot * sin_signed


def _layer_spec(block, layer):
    nd = len(block)
    return pl.BlockSpec((None,) + tuple(block), lambda *_: (layer,) + (0,) * nd, pipeline_mode=pl.Buffered(1))


def _const_spec(shape):
    nd = len(shape)
    return pl.BlockSpec(shape, lambda *_: (0,) * nd, pipeline_mode=pl.Buffered(1))


def _mod_spec(width, col_block, layer, row_of_step):
    return pl.BlockSpec((None, None, 1, width), lambda *idx: (layer, row_of_step(*idx), 0, col_block))


def _ada_kernel(c_ref, w_ref, b_ref, o_ref):
    c = c_ref[...]
    s = (c * jax.nn.sigmoid(c)).astype(BF16)
    o_ref[...] = _dot(s, w_ref[...].astype(BF16)) + b_ref[...]


def _ada_call(c_all, w_ada, b_ada):
    depth = w_ada.shape[0]
    tn = 1536
    n_out = N_MOD * D_MODEL
    return pl.pallas_call(
        _ada_kernel,
        out_shape=jax.ShapeDtypeStruct((depth, MOD_ROWS, n_out), F32),
        grid=(depth, n_out // tn),
        in_specs=[
            pl.BlockSpec((MOD_ROWS, D_MODEL), lambda l, j: (0, 0)),
            pl.BlockSpec((None, D_MODEL, tn), lambda l, j: (l, 0, j)),
            pl.BlockSpec((None, 1, tn), lambda l, j: (l, 0, j)),
        ],
        out_specs=pl.BlockSpec((None, MOD_ROWS, tn), lambda l, j: (l, 0, j)),
        compiler_params=pltpu.CompilerParams(
            dimension_semantics=("arbitrary", "arbitrary"), vmem_limit_bytes=VMEM_LIMIT),
        name="ada_mod",
    )(c_all, w_ada, b_ada.reshape(depth, 1, n_out))


def _front_kernel(*refs, rope, cache_out, sub):
    (x_ref, mod_ref, n1_ref, win_ref, wtail_ref, sgun_ref, sguw_ref, sgub_ref, qn_ref, wq_ref, kvn_ref,
     wkv_ref) = refs[:12]
    pos = 12
    if rope:
        cos_s_ref, sin_s_ref, cos_m_ref, sin_m_ref = refs[pos:pos + 4]
        pos += 4
    if cache_out:
        pos += 4
    (ab_ref, pc_ref, ysgu_ref, qm_ref, kcat_ref, vm_ref, qs_ref, ks_ref, vs_ref) = refs[pos:pos + 9]
    pos += 9
    if cache_out:
        ckv_o, kpe_o, ksw_o, vsw_o = refs[pos:pos + 4]

        def put(o_ref, val, st):
            per = val.shape[0] // o_ref.shape[1]
            o_ref[st * per:(st + 1) * per] = val.reshape((per,) + o_ref.shape[1:])

    shift = mod_ref[:, 0:D_MODEL]
    scale = mod_ref[:, D_MODEL:2 * D_MODEL]
    head_of_lane = lax.broadcasted_iota(jnp.int32, (CHUNK, GROUP_W), 1) // HEAD_DIM

    for st in range(x_ref.shape[0] // sub):
        rows = slice(st * sub, (st + 1) * sub)
        h = (_rms(x_ref[rows, :], n1_ref[...]) * (1.0 + scale) + shift).astype(BF16)
        if rope:
            cos_s, sin_s = cos_s_ref[rows, :], sin_s_ref[rows, :]
            cos_m, sin_m = cos_m_ref[rows, :], sin_m_ref[rows, :]

        zq = _dot(h, win_ref[:, C_CQ:C_CQ + MLA_Q_LORA])
        zkv = _dot(h, wtail_ref[:, T_KV:T_KV + 2 * LANES])
        zs = _dot(h, win_ref[:, C_SGU:C_SGU + 2 * GROUP_W])

        cqn = _rms(zq, qn_ref[...]).astype(BF16)
        q = _dot(cqn, wq_ref[...]) * (MLA_SCALE * LOG2E)
        for hh in range(MLA_HEADS):
            qh = q[:, hh * LANES:(hh + 1) * LANES]
            if rope:
                qh = _rope(qh, cos_m, sin_m, MLA_ROPE // 4)
            qm_ref[rows, hh * LANES:(hh + 1) * LANES] = qh.astype(BF16)

        ckv = _rms(zkv[:, :MLA_KV_LORA], kvn_ref[...])
        kpe = zkv[:, MLA_KV_LORA:]
        if cache_out:
            put(ckv_o, ckv, st)
            put(kpe_o, kpe[:, :MLA_ROPE], st)
        if rope:
            kpe = _rope(kpe, cos_m, sin_m, MLA_ROPE // 4)
        kv = _dot(ckv.astype(BF16), wkv_ref[...])
        for hh in range(MLA_HEADS):
            kcat_ref[rows, hh * LANES:(hh + 1) * LANES] = (kv[:, hh * LANES:(hh + 1) * LANES] + kpe).astype(BF16)
        vm_ref[rows, :] = kv[:, MLA_HEADS * LANES:].astype(BF16)

        zc = _dot(h, win_ref[:, C_CONV:C_CONV + 3 * GROUP_W])
        ab_ref[rows, :] = zc[:, :GROUP_W]
        pc_ref[rows, :] = zc[:, GROUP_W:2 * GROUP_W] * zc[:, 2 * GROUP_W:]

        zw = _dot(h, wtail_ref[:, T_SWA:T_SWA + 4 * LANES])
        sk = zw[:, 2 * LANES:3 * LANES]
        sv_ = zw[:, 3 * LANES:]
        if cache_out:
            put(ksw_o, sk, st)
            put(vsw_o, sv_, st)
        for g in range(SWA_GROUP):
            qg = zw[:, g * LANES:(g + 1) * LANES] * (SWA_SCALE * LOG2E)
            if rope:
                qg = _rope(qg, cos_s, sin_s, HEAD_DIM // 4)
            qs_ref[rows, g * LANES:(g + 1) * LANES] = qg.astype(BF16)
        if rope:
            sk = _rope(sk, cos_s, sin_s, HEAD_DIM // 4)
        ks_ref[rows, :] = sk.astype(BF16)
        vs_ref[rows, :] = sv_.astype(BF16)

        gz = jax.nn.gelu(zs, approximate=True)
        u = gz[:, :GROUP_W]
        vn = _rms(gz[:, GROUP_W:], sgun_ref[...])
        for c in range(sub // CHUNK):
            crow = slice(c * CHUNK, (c + 1) * CHUNK)
            vc = vn[crow]
            stacked = jnp.concatenate(
                [jnp.where(head_of_lane == hh, vc, 0.0).astype(BF16) for hh in range(SGU_HEADS)], axis=0)
            sv = _dot(sguw_ref[...], stacked) + sgub_ref[...]
            ysgu_ref[st * sub + c * CHUNK:st * sub + (c + 1) * CHUNK, :] = (u[crow] * sv).astype(BF16)


def _front_call(x, mod, pw, tables, cache_stacks, *, layer, seq, tile, sub, mod_row0):
    n = x.shape[0]
    rope = tables is not None
    tiles_per_seq = max(seq // tile, 1)

    def row_spec(width):
        return pl.BlockSpec((tile, width), lambda i: (i, 0))

    def mod_row(i):
        return mod_row0 + (i * tile) // seq if rope else mod_row0

    in_specs = [
        row_spec(D_MODEL),
        _mod_spec(2 * D_MODEL, 0, layer, mod_row),
        _layer_spec((1, D_MODEL), layer),
        _layer_spec((D_MODEL, IN_MAIN), layer),
        _layer_spec((D_MODEL, IN_TAIL), layer),
        _layer_spec((1, GROUP_W), layer),
        _layer_spec((CHUNK, SGU_HEADS * CHUNK), layer),
        _layer_spec((CHUNK, GROUP_W), layer),
        _layer_spec((1, MLA_Q_LORA), layer),
        _layer_spec((MLA_Q_LORA, MLA_HEADS * LANES), layer),
        _layer_spec((1, MLA_KV_LORA), layer),
        _layer_spec((MLA_KV_LORA, KV_UP_COLS), layer),
    ]
    args = [x, mod, pw["norm1"], pw["w_in"], pw["w_in_tail"], pw["sgu_norm"], pw["sgu_w"], pw["sgu_b"],
            pw["q_norm"], pw["w_q_up"], pw["kv_norm"], pw["w_kv_up"]]
    if rope:
        in_specs += [pl.BlockSpec((tile, LANES), lambda i: (i % tiles_per_seq, 0))] * 4
        args += list(tables)

    out_shape = [
        jax.ShapeDtypeStruct((n, GROUP_W), F32),
        jax.ShapeDtypeStruct((n, GROUP_W), F32),
        jax.ShapeDtypeStruct((n, GROUP_W), BF16),
        jax.ShapeDtypeStruct((n, MLA_HEADS * LANES), BF16),
        jax.ShapeDtypeStruct((n, MLA_HEADS * LANES), BF16),
        jax.ShapeDtypeStruct((n, MLA_HEADS * MLA_V), BF16),
        jax.ShapeDtypeStruct((n, 2 * LANES), BF16),
        jax.ShapeDtypeStruct((n, LANES), BF16),
        jax.ShapeDtypeStruct((n, LANES), BF16),
    ]
    out_specs = [row_spec(s.shape[1]) for s in out_shape]
    aliases = {}
    if cache_stacks is not None:
        assert tile % seq == 0
        for stack in cache_stacks:
            aliases[len(args)] = len(out_shape)
            in_specs.append(pl.BlockSpec(memory_space=pl.ANY))
            args.append(stack)
            out_shape.append(jax.ShapeDtypeStruct(stack.shape, stack.dtype))
            out_specs.append(pl.BlockSpec((tile // seq, None, seq, stack.shape[3]), lambda i: (i, layer, 0, 0)))
    return pl.pallas_call(
        functools.partial(_front_kernel, rope=rope, cache_out=cache_stacks is not None, sub=sub),
        out_shape=out_shape,
        grid=(n // tile,),
        in_specs=in_specs,
        out_specs=out_specs,
        input_output_aliases=aliases,
        compiler_params=pltpu.CompilerParams(
            dimension_semantics=("arbitrary",), vmem_limit_bytes=VMEM_LIMIT),
        name="front_lat" if rope else "front_ctx",
    )(*args)


def _scores(q, pieces):
    scores = []
    for k, _, mask in pieces:
        s = _dot_nt(q, k)
        if mask is not None:
            s = jnp.where(mask, s, NEG_INF)
        scores.append(s)
    return scores


def _softmax(scores, sink):
    m = functools.reduce(jnp.maximum, [jnp.max(s, axis=-1, keepdims=True) for s in scores])
    if sink is not None:
        m = jnp.maximum(m, sink)
    denom = jnp.exp2(sink - m) if sink is not None else jnp.zeros_like(m)
    probs = []
    for s in scores:
        p = jnp.exp2(s - m)
        denom = denom + jnp.sum(p, axis=-1, keepdims=True)
        probs.append(p.astype(BF16))
    return probs, 1.0 / denom


def _pv(probs, inv_denom, pieces):
    out = None
    for p, (_, v, _) in zip(probs, pieces):
        pv = _dot(p, v)
        out = pv if out is None else out + pv
    return out * inv_denom


def _attn_kernel(*refs, latent, tq, sub, seq, layer):
    sink_ref, qm_ref, kcat_ref, vm_ref, qs_ref, ks_ref, vs_ref = refs[:7]
    pos = 7
    if latent:
        cckv_ref, ckpe_ref, cks_ref, cvs_ref, wkv_ref, place_ref = refs[pos:pos + 6]
        pos += 6
    ym_ref, ys_ref = refs[pos:pos + 2]
    pos += 2
    if latent:
        kc_scr, vc_scr, ksc_scr, vsc_scr = refs[pos:pos + 4]
        qi = pl.program_id(1)

        @pl.when(qi == 0)
        def _expand_cache():
            kv = _dot(cckv_ref[...].astype(BF16), wkv_ref[...])
            kp = _dot(ckpe_ref[...].astype(BF16), place_ref[...])
            for hh in range(MLA_HEADS):
                kc_scr[:, hh * LANES:(hh + 1) * LANES] = (kv[:, hh * LANES:(hh + 1) * LANES] + kp).astype(BF16)
            vc_scr[...] = kv[:, MLA_HEADS * LANES:].astype(BF16)
            ksc_scr[...] = cks_ref[...].astype(BF16)
            vsc_scr[...] = cvs_ref[...].astype(BF16)

    tasks = []
    kv_of_lane = lax.broadcasted_iota(jnp.int32, (sub, LANES), 1) // HEAD_DIM
    for sb in range(tq // sub):
        qrows = slice(sb * sub, (sb + 1) * sub)
        for hh in range(MLA_HEADS):
            cols = slice(hh * LANES, (hh + 1) * LANES)
            pieces = [(kcat_ref[:, cols], vm_ref[...], None)]
            if latent:
                pieces.append((kc_scr[:, cols], vc_scr[...], None))
            tasks.append((qm_ref[qrows, cols], pieces, None))
        if latent:
            win = sub + 2 * SWA_WINDOW
            q0 = qi * tq + sb * sub
            start = pl.multiple_of(jnp.clip(q0 - SWA_WINDOW, 0, seq - win), SWA_WINDOW)
            q_pos = q0 + lax.broadcasted_iota(jnp.int32, (sub, win), 0)
            k_pos = start + lax.broadcasted_iota(jnp.int32, (sub, win), 1)
            band_ok = jnp.abs(q_pos - k_pos) <= SWA_WINDOW
            swa_pieces = [(ks_ref[pl.ds(start, win), :], vs_ref[pl.ds(start, win), :], band_ok),
                          (ksc_scr[...], vsc_scr[...], None)]
        else:
            swa_pieces = [(ks_ref[...], vs_ref[...], None)]
        for g in range(SWA_GROUP):
            qg = qs_ref[qrows, g * LANES:(g + 1) * LANES].astype(F32)
            for n in range(SWA_KV_HEADS):
                qh = jnp.where(kv_of_lane == n, qg, 0.0).astype(BF16)
                tasks.append((qh, swa_pieces, sink_ref[layer, n * SWA_GROUP + g] * LOG2E))

    per_block = MLA_HEADS + SWA_GROUP * SWA_KV_HEADS
    head_of_lane = lax.broadcasted_iota(jnp.int32, (sub, MLA_HEADS * MLA_V), 1) // MLA_V
    om = jnp.zeros((sub, MLA_HEADS * MLA_V), F32)
    blk = jnp.zeros((sub, LANES), F32)
    scores = _scores(tasks[0][0], tasks[0][1])
    for i, (_, pieces, sink) in enumerate(tasks):
        probs, inv_denom = _softmax(scores, sink)
        scores = _scores(tasks[i + 1][0], tasks[i + 1][1]) if i + 1 < len(tasks) else None
        o = _pv(probs, inv_denom, pieces)
        sb, j = divmod(i, per_block)
        qrows = slice(sb * sub, (sb + 1) * sub)
        if j < MLA_HEADS:
            om = jnp.where(head_of_lane == j, o, om)
            if j == MLA_HEADS - 1:
                ym_ref[qrows, :] = om.astype(BF16)
        else:
            g, n = divmod(j - MLA_HEADS, SWA_KV_HEADS)
            blk = jnp.where(kv_of_lane == n, o, blk)
            if n == SWA_KV_HEADS - 1:
                ys_ref[qrows, g * LANES:(g + 1) * LANES] = blk.astype(BF16)


def _attn_call(front, sink, pw, cache, *, batch, seq, tq, sub, layer):
    qm, kcat, vm, qs, ks, vs = front
    latent = cache is not None
    n_q = seq // tq

    def q_spec(width):
        return pl.BlockSpec((tq, width), lambda b, i: (b * n_q + i, 0))

    def seq_spec(width):
        return pl.BlockSpec((seq, width), lambda b, i: (b, 0))

    in_specs = [
        pl.BlockSpec(memory_space=pltpu.SMEM),
        q_spec(MLA_HEADS * LANES), seq_spec(MLA_HEADS * LANES), seq_spec(MLA_HEADS * MLA_V),
        q_spec(2 * LANES), seq_spec(LANES), seq_spec(LANES),
    ]
    args = [sink, qm, kcat, vm, qs, ks, vs]
    scratch = []
    if latent:
        c_ckv, c_kpe, c_k, c_v = cache
        past = c_ckv.shape[2]

        def cache_spec(width):
            return pl.BlockSpec((None, None, past, width), lambda b, i: (b, layer, 0, 0))

        in_specs += [cache_spec(MLA_KV_LORA), cache_spec(MLA_ROPE), cache_spec(LANES), cache_spec(LANES),
                     _layer_spec((MLA_KV_LORA, KV_UP_COLS), layer),
                     _const_spec((MLA_ROPE, LANES))]
        args += [c_ckv, c_kpe, c_k, c_v, pw["w_kv_up"], pw["kpe_place"]]
        scratch = [pltpu.VMEM((past, MLA_HEADS * LANES), BF16), pltpu.VMEM((past, MLA_HEADS * MLA_V), BF16),
                   pltpu.VMEM((past, LANES), BF16), pltpu.VMEM((past, LANES), BF16)]
    n = batch * seq
    return pl.pallas_call(
        functools.partial(_attn_kernel, latent=latent, tq=tq, sub=sub, seq=seq, layer=layer),
        out_shape=[jax.ShapeDtypeStruct((n, MLA_HEADS * MLA_V), BF16),
                   jax.ShapeDtypeStruct((n, 2 * LANES), BF16)],
        grid=(batch, n_q),
        in_specs=in_specs,
        out_specs=[q_spec(MLA_HEADS * MLA_V), q_spec(2 * LANES)],
        scratch_shapes=scratch,
        compiler_params=pltpu.CompilerParams(
            dimension_semantics=("arbitrary", "arbitrary"), vmem_limit_bytes=VMEM_LIMIT),
        name="attn_lat" if latent else "attn_ctx",
    )(*args)


def _back_kernel(x_ref, mod_ref, ab_ref, pc_ref, pprev_ref, pnext_ref, ysgu_ref, ym_ref, ys_ref, convw_ref,
                 wout_ref, n2_ref, w1_ref, w2_ref, fn_ref, o_ref, *, seq, sub, final, hidden_chunk):
    t = x_ref.shape[0]
    row = pl.program_id(0) * t + lax.broadcasted_iota(jnp.int32, (t, GROUP_W), 0)
    local = lax.broadcasted_iota(jnp.int32, (t, GROUP_W), 0)
    pc = pc_ref[...]
    prev_row = pprev_ref[SUBLANES - 1:SUBLANES, :]
    next_row = pnext_ref[0:1, :]
    up = jnp.where(local == 0, prev_row, pltpu.roll(pc, 1, 0))
    up = jnp.where(row % seq == 0, 0.0, up)
    dn = jnp.where(local == t - 1, next_row, pltpu.roll(pc, t - 1, 0))
    dn = jnp.where((row + 1) % seq == 0, 0.0, dn)
    conv = convw_ref[0:1, :] * up + convw_ref[1:2, :] * pc + convw_ref[2:3, :] * dn
    y_conv = (ab_ref[...] * conv).astype(BF16)

    y = jnp.concatenate([y_conv, ysgu_ref[...], ym_ref[...], ys_ref[...]], axis=1)
    gate1 = mod_ref[:, 2 * D_MODEL:3 * D_MODEL]
    shift2 = mod_ref[:, 3 * D_MODEL:4 * D_MODEL]
    scale2 = mod_ref[:, 4 * D_MODEL:5 * D_MODEL]
    gate2 = mod_ref[:, 5 * D_MODEL:6 * D_MODEL]
    subs = [slice(k * sub, (k + 1) * sub) for k in range(t // sub)]
    x1s = [x_ref[r, :] + gate1 * _dot(y[r], wout_ref[...]) for r in subs]
    for r, x1 in zip(subs, x1s):
        h2 = (_rms(x1, n2_ref[...]) * (1.0 + scale2) + shift2).astype(BF16)
        acc = jnp.zeros((sub, D_MODEL), F32)
        for c in range(MLP_HIDDEN // hidden_chunk):
            cols = slice(c * hidden_chunk, (c + 1) * hidden_chunk)
            hid = jnp.maximum(_dot(h2, w1_ref[:, cols]), 0.0)
            acc = acc + _dot((hid * hid).astype(BF16), w2_ref[cols, :])
        x2 = x1 + gate2 * acc
        if final:
            x2 = _rms(x2, fn_ref[...])
        o_ref[r, :] = x2


def _back_call(x, mod, ab, pc, ysgu, ym, ys, pw, final_norm, *, layer, seq, tile, sub, mod_row0, per_seq_mod,
               final):
    n = x.shape[0]
    halo_blocks = n // SUBLANES
    per_tile = tile // SUBLANES

    def row_spec(width):
        return pl.BlockSpec((tile, width), lambda i: (i, 0))

    def mod_row(i):
        return mod_row0 + (i * tile) // seq if per_seq_mod else mod_row0

    in_specs = [
        row_spec(D_MODEL),
        _mod_spec(N_MOD * D_MODEL, 0, layer, mod_row),
        row_spec(GROUP_W), row_spec(GROUP_W),
        pl.BlockSpec((SUBLANES, GROUP_W), lambda i: (jnp.maximum(i * per_tile - 1, 0), 0)),
        pl.BlockSpec((SUBLANES, GROUP_W), lambda i: (jnp.minimum((i + 1) * per_tile, halo_blocks - 1), 0)),
        row_spec(GROUP_W), row_spec(GROUP_W), row_spec(GROUP_W),
        _layer_spec((CONV_WIDTH, GROUP_W), layer),
        _layer_spec((D_MODEL, D_MODEL), layer),
        _layer_spec((1, D_MODEL), layer),
        _layer_spec((D_MODEL, MLP_HIDDEN), layer),
        _layer_spec((MLP_HIDDEN, D_MODEL), layer),
        _const_spec((1, D_MODEL)),
    ]
    return pl.pallas_call(
        functools.partial(_back_kernel, seq=seq, sub=sub, final=final, hidden_chunk=1024),
        out_shape=jax.ShapeDtypeStruct((n, D_MODEL), F32),
        grid=(n // tile,),
        in_specs=in_specs,
        out_specs=row_spec(D_MODEL),
        compiler_params=pltpu.CompilerParams(
            dimension_semantics=("arbitrary",), vmem_limit_bytes=VMEM_LIMIT),
        name="back",
    )(x, mod, ab, pc, pc, pc, ysgu, ym, ys, pw["conv_w"], pw["w_out"], pw["norm2"], pw["w1"], pw["w2"],
      final_norm)


def _rope_tables(n_tokens, rot_dim):
    rows = n_tokens // GRID_W
    row = jnp.repeat(jnp.arange(rows, dtype=F32), GRID_W)
    col = jnp.tile(jnp.arange(GRID_W, dtype=F32), rows)
    half = rot_dim // 2
    inv_freq = ROPE_THETA ** (-jnp.arange(0, half, 2, dtype=F32) / half)
    ang_r = row[:, None] * inv_freq[None, :]
    ang_c = col[:, None] * inv_freq[None, :]
    ang = jnp.concatenate([ang_r, ang_r, ang_c, ang_c], axis=-1)
    sign = jnp.where((jnp.arange(rot_dim) & (rot_dim // 4)) == 0, -1.0, 1.0).astype(F32)
    return jnp.cos(ang), jnp.sin(ang) * sign[None, :]


def _lane_tables(n_tokens):
    cos_s, sin_s = _rope_tables(n_tokens, HEAD_DIM)
    cos_s = jnp.tile(cos_s, (1, LANES // HEAD_DIM))
    sin_s = jnp.tile(sin_s, (1, LANES // HEAD_DIM))
    cos_m, sin_m = _rope_tables(n_tokens, MLA_ROPE)
    pad = LANES - MLA_ROPE
    cos_m = jnp.concatenate([cos_m, jnp.ones((n_tokens, pad), F32)], axis=1)
    sin_m = jnp.concatenate([sin_m, jnp.zeros((n_tokens, pad), F32)], axis=1)
    return cos_s, sin_s, cos_m, sin_m


def _prep_weights(norm1, norm2, w_in, conv_w, sgu_norm, sgu_w, sgu_b, mla_q_norm, mla_w_q_up, mla_kv_norm,
                  mla_w_kv_up, w_out, mlp_w1, mlp_w2):
    depth, d, _ = w_in.shape
    s_kpe = IN_MAIN + MLA_KV_LORA
    s_q = s_kpe + MLA_ROPE
    w_in_main = w_in[:, :, :IN_MAIN].astype(BF16)
    sq = w_in[:, :, s_q:s_q + 4 * HEAD_DIM].reshape(depth, d, SWA_KV_HEADS, SWA_GROUP, HEAD_DIM)
    sq = sq.transpose(0, 1, 3, 2, 4).reshape(depth, d, 4 * HEAD_DIM)
    w_in_tail = jnp.concatenate(
        [w_in[:, :, IN_MAIN:s_q], jnp.zeros((depth, d, LANES - MLA_ROPE), F32), sq,
         w_in[:, :, s_q + 4 * HEAD_DIM:]], axis=2).astype(BF16)

    wq = mla_w_q_up.reshape(depth, MLA_Q_LORA, MLA_HEADS, MLA_NOPE + MLA_ROPE)
    wq_p = jnp.concatenate(
        [wq[..., MLA_NOPE:], wq[..., :MLA_NOPE],
         jnp.zeros((depth, MLA_Q_LORA, MLA_HEADS, LANES - MLA_NOPE - MLA_ROPE), F32)], axis=-1)
    wq_p = wq_p.reshape(depth, MLA_Q_LORA, MLA_HEADS * LANES).astype(BF16)

    wkv = mla_w_kv_up.reshape(depth, MLA_KV_LORA, MLA_HEADS, MLA_NOPE + MLA_V)
    wk_p = jnp.concatenate(
        [jnp.zeros((depth, MLA_KV_LORA, MLA_HEADS, MLA_ROPE), F32), wkv[..., :MLA_NOPE],
         jnp.zeros((depth, MLA_KV_LORA, MLA_HEADS, LANES - MLA_NOPE - MLA_ROPE), F32)], axis=-1)
    wkv_p = jnp.concatenate(
        [wk_p.reshape(depth, MLA_KV_LORA, MLA_HEADS * LANES),
         wkv[..., MLA_NOPE:].reshape(depth, MLA_KV_LORA, MLA_HEADS * MLA_V)], axis=2).astype(BF16)

    sguw_p = sgu_w.transpose(0, 2, 1, 3).reshape(depth, CHUNK, SGU_HEADS * CHUNK).astype(BF16)
    sgub_p = jnp.repeat(sgu_b.transpose(0, 2, 1), HEAD_DIM, axis=2)

    wo_swa = w_out[:, 3 * GROUP_W:].reshape(depth, SWA_KV_HEADS, SWA_GROUP, HEAD_DIM, D_MODEL)
    wo_swa = wo_swa.transpose(0, 2, 1, 3, 4).reshape(depth, GROUP_W, D_MODEL)
    wo_p = jnp.concatenate([w_out[:, :3 * GROUP_W], wo_swa], axis=1).astype(BF16)

    return {
        "norm1": norm1[:, None, :], "norm2": norm2[:, None, :], "w_in": w_in_main, "w_in_tail": w_in_tail,
        "conv_w": conv_w,
        "sgu_norm": sgu_norm[:, None, :], "sgu_w": sguw_p, "sgu_b": sgub_p,
        "q_norm": mla_q_norm[:, None, :], "w_q_up": wq_p, "kv_norm": mla_kv_norm[:, None, :],
        "w_kv_up": wkv_p, "kpe_place": jnp.eye(MLA_ROPE, LANES, dtype=BF16), "w_out": wo_p,
        "w1": mlp_w1.astype(BF16), "w2": mlp_w2.astype(BF16),
    }


def _pick_tile(n, pref):
    while n % pref:
        pref //= 2
    return pref


def kernel(x_prompt, x_sample, cache_mla_ckv, cache_mla_kpe, cache_swa_k, cache_swa_v, c, c_ctx, w_ada, b_ada,
           norm1, norm2, w_in, conv_w, sgu_norm, sgu_w, sgu_b, mla_q_norm, mla_w_q_up, mla_kv_norm, mla_w_kv_up,
           swa_sink, w_out, mlp_w1, mlp_w2, final_norm):
    batch, seq, d = x_prompt.shape
    dec_batch, dec_seq, _ = x_sample.shape
    depth = w_ada.shape[0]
    past = cache_mla_ckv.shape[2]
    assert d == D_MODEL and seq % CHUNK == 0 and dec_seq % (2 * CHUNK) == 0 and 1 + dec_batch <= MOD_ROWS

    c_all = jnp.concatenate(
        [c_ctx[None, :], c, jnp.zeros((MOD_ROWS - 1 - dec_batch, d), F32)], axis=0)
    mod = _ada_call(c_all, w_ada, b_ada).reshape(depth, MOD_ROWS, 1, N_MOD * d)

    pw = _prep_weights(norm1, norm2, w_in, conv_w, sgu_norm, sgu_w, sgu_b, mla_q_norm, mla_w_q_up, mla_kv_norm,
                       mla_w_kv_up, w_out, mlp_w1, mlp_w2)
    tables = _lane_tables(dec_seq)
    cache = (cache_mla_ckv, cache_mla_kpe,
             cache_swa_k.reshape(dec_batch, depth, past, LANES), cache_swa_v.reshape(dec_batch, depth, past, LANES))
    fn = final_norm[None, :]

    xp = x_prompt.reshape(batch * seq, d)
    xs = x_sample.reshape(dec_batch * dec_seq, d)
    tile_p = _pick_tile(batch * seq, 512)
    tile_s = _pick_tile(dec_seq, 512)
    tq_s = _pick_tile(dec_seq, 256)
    ftile_p = _pick_tile(batch * seq, 2 * tile_p)
    ftile_s = _pick_tile(dec_seq, 2 * tile_s)
    atile_s = _pick_tile(dec_seq, 2 * tq_s)
    bsub = 256
    new_cache = [jnp.zeros((batch, depth, seq, w), F32) for w in (MLA_KV_LORA, MLA_ROPE, LANES, LANES)]
    for l in range(depth):
        final = l == depth - 1

        fo = _front_call(xp, mod, pw, None, new_cache, layer=l, seq=seq, tile=ftile_p, sub=tile_p, mod_row0=0)
        ab, pc, ysgu = fo[:3]
        new_cache = list(fo[9:13])
        ym, ys = _attn_call(fo[3:9], swa_sink, pw, None, batch=batch, seq=seq, tq=seq, sub=seq, layer=l)
        xp = _back_call(xp, mod, ab, pc, ysgu, ym, ys, pw, fn, layer=l, seq=seq, tile=ftile_p, sub=bsub,
                        mod_row0=0, per_seq_mod=False, final=final)

        fo = _front_call(xs, mod, pw, tables, None, layer=l, seq=dec_seq, tile=ftile_s, sub=tile_s, mod_row0=1)
        ab, pc, ysgu = fo[:3]
        ym, ys = _attn_call(fo[3:9], swa_sink, pw, cache, batch=dec_batch, seq=dec_seq, tq=atile_s, sub=tq_s,
                            layer=l)
        xs = _back_call(xs, mod, ab, pc, ysgu, ym, ys, pw, fn, layer=l, seq=dec_seq, tile=ftile_s, sub=bsub,
                        mod_row0=1, per_seq_mod=True, final=final)

    swa_shape = (batch, depth, seq, SWA_KV_HEADS, HEAD_DIM)
    return (xp.reshape(batch, seq, d), xs.reshape(dec_batch, dec_seq, d), new_cache[0], new_cache[1],
            new_cache[2].reshape(swa_shape), new_cache[3].reshape(swa_shape))
```

```python
import functools

import jax
import jax.numpy as jnp
from jax import lax
from jax.experimental import pallas as pl
from jax.experimental.pallas import tpu as pltpu

D_MODEL = 1024
GROUP_W = 256
HEAD_DIM = 64
CONV_WIDTH = 3
CHUNK = 128
SGU_HEADS = 4
MLA_HEADS = 4
MLA_NOPE = 64
MLA_ROPE = 32
MLA_V = 64
MLA_Q_LORA = 256
MLA_KV_LORA = 128
SWA_KV_HEADS = 2
SWA_GROUP = 2
SWA_WINDOW = 128
MLP_HIDDEN = 4096
GRID_W = 64
ROPE_THETA = 10000.0
EPS = 1e-6
N_MOD = 6
MLA_SCALE = (MLA_NOPE + MLA_ROPE) ** -0.5
SWA_SCALE = HEAD_DIM ** -0.5
NEG_INF = -1e30
LOG2E = 1.4426950408889634

LANES = 128
SUBLANES = 8
MOD_ROWS = 16
VMEM_LIMIT = 56 * 1024 * 1024
ROW_SUB = 512
QUERY_SUB = 256
BACK_SUB = 256
CTX_SEQS_PER_STEP = 4
ADA_COLS = 1536
HIDDEN_CHUNK = 1024

C_CONV = 0
C_SGU = 768
C_CQ = 1280
IN_MAIN = 1536
T_KV = 0
T_SWA = 256
IN_TAIL = 768
KV_UP_COLS = MLA_HEADS * LANES + MLA_HEADS * MLA_V

BF16 = jnp.bfloat16
F32 = jnp.float32


def _dot(a, b):
    return jnp.dot(a, b, preferred_element_type=F32)


def _dot_nt(a, b):
    return lax.dot_general(a, b, (((1,), (1,)), ((), ())), preferred_element_type=F32)


def _rms(x, g):
    return x * lax.rsqrt(jnp.mean(x * x, axis=-1, keepdims=True) + EPS) * g


def _rope(x, cos, sin_signed, quarter):
    lane = lax.broadcasted_iota(jnp.int32, x.shape, 1)
    first = (lane & quarter) == 0
    rot = jnp.where(first, pltpu.roll(x, LANES - quarter, 1), pltpu.roll(x, quarter, 1))
    return x * cos + rot * sin_signed


def _layer_spec(block, layer):
    nd = len(block)
    return pl.BlockSpec((None,) + tuple(block), lambda *_: (layer,) + (0,) * nd, pipeline_mode=pl.Buffered(1))


def _const_spec(shape):
    nd = len(shape)
    return pl.BlockSpec(shape, lambda *_: (0,) * nd, pipeline_mode=pl.Buffered(1))


def _mod_spec(width, col_block, layer, row_of_step):
    return pl.BlockSpec((None, None, 1, width), lambda *idx: (layer, row_of_step(*idx), 0, col_block))


def _ada_kernel(c_ref, w_ref, b_ref, o_ref):
    c = c_ref[...]
    s = (c * jax.nn.sigmoid(c)).astype(BF16)
    o_ref[...] = _dot(s, w_ref[...].astype(BF16)) + b_ref[...]


def _ada_call(c_all, w_ada, b_ada):
    depth = w_ada.shape[0]
    tn = ADA_COLS
    n_out = N_MOD * D_MODEL
    return pl.pallas_call(
        _ada_kernel,
        out_shape=jax.ShapeDtypeStruct((depth, MOD_ROWS, n_out), F32),
        grid=(depth, n_out // tn),
        in_specs=[
            pl.BlockSpec((MOD_ROWS, D_MODEL), lambda l, j: (0, 0)),
            pl.BlockSpec((None, D_MODEL, tn), lambda l, j: (l, 0, j)),
            pl.BlockSpec((None, 1, tn), lambda l, j: (l, 0, j)),
        ],
        out_specs=pl.BlockSpec((None, MOD_ROWS, tn), lambda l, j: (l, 0, j)),
        compiler_params=pltpu.CompilerParams(
            dimension_semantics=("arbitrary", "arbitrary"), vmem_limit_bytes=VMEM_LIMIT),
        name="ada_mod",
    )(c_all, w_ada, b_ada.reshape(depth, 1, n_out))


def _front_kernel(*refs, rope, cache_mode, sub):
    (x_ref, mod_ref, n1_ref, win_ref, wtail_ref, sgun_ref, sguw_ref, sgub_ref, qn_ref, wq_ref, kvn_ref,
     wkv_ref) = refs[:12]
    pos = 12
    if rope:
        cos_s_ref, sin_s_ref, cos_m_ref, sin_m_ref = refs[pos:pos + 4]
        pos += 4
    cache_out = cache_mode is not None
    if cache_mode == "update":
        pos += 4
    (ab_ref, pc_ref, ysgu_ref, qm_ref, kcat_ref, vm_ref, qs_ref, ks_ref, vs_ref) = refs[pos:pos + 9]
    pos += 9
    if cache_out:
        ckv_o, kpe_o, ksw_o, vsw_o = refs[pos:pos + 4]
        if cache_mode == "init":
            for o_ref in (ckv_o, kpe_o, ksw_o, vsw_o):
                o_ref[:, 1:] = jnp.zeros((o_ref.shape[0], o_ref.shape[1] - 1) + o_ref.shape[2:], F32)

        def put(o_ref, val, st):
            per = val.shape[0] // o_ref.shape[-2]
            slab = val.reshape((per,) + o_ref.shape[-2:])
            if cache_mode == "init":
                o_ref[st * per:(st + 1) * per, 0] = slab
            else:
                o_ref[st * per:(st + 1) * per] = slab

    shift = mod_ref[:, 0:D_MODEL]
    scale = mod_ref[:, D_MODEL:2 * D_MODEL]
    head_of_lane = lax.broadcasted_iota(jnp.int32, (CHUNK, GROUP_W), 1) // HEAD_DIM

    for st in range(x_ref.shape[0] // sub):
        rows = slice(st * sub, (st + 1) * sub)
        h = (_rms(x_ref[rows, :], n1_ref[...]) * (1.0 + scale) + shift).astype(BF16)
        if rope:
            cos_s, sin_s = cos_s_ref[rows, :], sin_s_ref[rows, :]
            cos_m, sin_m = cos_m_ref[rows, :], sin_m_ref[rows, :]

        zq = _dot(h, win_ref[:, C_CQ:C_CQ + MLA_Q_LORA])
        zkv = _dot(h, wtail_ref[:, T_KV:T_KV + 2 * LANES])
        zs = _dot(h, win_ref[:, C_SGU:C_SGU + 2 * GROUP_W])

        cqn = _rms(zq, qn_ref[...]).astype(BF16)
        q = _dot(cqn, wq_ref[...]) * (MLA_SCALE * LOG2E)
        for hh in range(MLA_HEADS):
            qh = q[:, hh * LANES:(hh + 1) * LANES]
            if rope:
                qh = _rope(qh, cos_m, sin_m, MLA_ROPE // 4)
            qm_ref[rows, hh * LANES:(hh + 1) * LANES] = qh.astype(BF16)

        ckv = _rms(zkv[:, :MLA_KV_LORA], kvn_ref[...])
        kpe = zkv[:, MLA_KV_LORA:]
        if cache_out:
            put(ckv_o, ckv, st)
            put(kpe_o, kpe[:, :MLA_ROPE], st)
        if rope:
            kpe = _rope(kpe, cos_m, sin_m, MLA_ROPE // 4)
        kv = _dot(ckv.astype(BF16), wkv_ref[...])
        for hh in range(MLA_HEADS):
            kcat_ref[rows, hh * LANES:(hh + 1) * LANES] = (kv[:, hh * LANES:(hh + 1) * LANES] + kpe).astype(BF16)
        vm_ref[rows, :] = kv[:, MLA_HEADS * LANES:].astype(BF16)

        zc = _dot(h, win_ref[:, C_CONV:C_CONV + 3 * GROUP_W])
        ab_ref[rows, :] = zc[:, :GROUP_W]
        pc_ref[rows, :] = zc[:, GROUP_W:2 * GROUP_W] * zc[:, 2 * GROUP_W:]

        zw = _dot(h, wtail_ref[:, T_SWA:T_SWA + 4 * LANES])
        sk = zw[:, 2 * LANES:3 * LANES]
        sv_ = zw[:, 3 * LANES:]
        if cache_out:
            put(ksw_o, sk, st)
            put(vsw_o, sv_, st)
        for g in range(SWA_GROUP):
            qg = zw[:, g * LANES:(g + 1) * LANES] * (SWA_SCALE * LOG2E)
            if rope:
                qg = _rope(qg, cos_s, sin_s, HEAD_DIM // 4)
            qs_ref[rows, g * LANES:(g + 1) * LANES] = qg.astype(BF16)
        if rope:
            sk = _rope(sk, cos_s, sin_s, HEAD_DIM // 4)
        ks_ref[rows, :] = sk.astype(BF16)
        vs_ref[rows, :] = sv_.astype(BF16)

        gz = jax.nn.gelu(zs, approximate=True)
        u = gz[:, :GROUP_W]
        vn = _rms(gz[:, GROUP_W:], sgun_ref[...])
        for c in range(sub // CHUNK):
            crow = slice(c * CHUNK, (c + 1) * CHUNK)
            vc = vn[crow]
            stacked = jnp.concatenate(
                [jnp.where(head_of_lane == hh, vc, 0.0).astype(BF16) for hh in range(SGU_HEADS)], axis=0)
            sv = _dot(sguw_ref[...], stacked) + sgub_ref[...]
            ysgu_ref[st * sub + c * CHUNK:st * sub + (c + 1) * CHUNK, :] = (u[crow] * sv).astype(BF16)


def _front_call(x, mod, pw, tables, cache_stacks, *, layer, seq, tile, sub, mod_row0):
    n = x.shape[0]
    rope = tables is not None
    tiles_per_seq = max(seq // tile, 1)

    def row_spec(width):
        return pl.BlockSpec((tile, width), lambda i: (i, 0))

    def mod_row(i):
        return mod_row0 + (i * tile) // seq if rope else mod_row0

    in_specs = [
        row_spec(D_MODEL),
        _mod_spec(2 * D_MODEL, 0, layer, mod_row),
        _layer_spec((1, D_MODEL), layer),
        _layer_spec((D_MODEL, IN_MAIN), layer),
        _layer_spec((D_MODEL, IN_TAIL), layer),
        _layer_spec((1, GROUP_W), layer),
        _layer_spec((CHUNK, SGU_HEADS * CHUNK), layer),
        _layer_spec((CHUNK, GROUP_W), layer),
        _layer_spec((1, MLA_Q_LORA), layer),
        _layer_spec((MLA_Q_LORA, MLA_HEADS * LANES), layer),
        _layer_spec((1, MLA_KV_LORA), layer),
        _layer_spec((MLA_KV_LORA, KV_UP_COLS), layer),
    ]
    args = [x, mod, pw["norm1"], pw["w_in"], pw["w_in_tail"], pw["sgu_norm"], pw["sgu_w"], pw["sgu_b"],
            pw["q_norm"], pw["w_q_up"], pw["kv_norm"], pw["w_kv_up"]]
    if rope:
        in_specs += [pl.BlockSpec((tile, LANES), lambda i: (i % tiles_per_seq, 0))] * 4
        args += list(tables)

    out_shape = [
        jax.ShapeDtypeStruct((n, GROUP_W), F32),
        jax.ShapeDtypeStruct((n, GROUP_W), F32),
        jax.ShapeDtypeStruct((n, GROUP_W), BF16),
        jax.ShapeDtypeStruct((n, MLA_HEADS * LANES), BF16),
        jax.ShapeDtypeStruct((n, MLA_HEADS * LANES), BF16),
        jax.ShapeDtypeStruct((n, MLA_HEADS * MLA_V), BF16),
        jax.ShapeDtypeStruct((n, 2 * LANES), BF16),
        jax.ShapeDtypeStruct((n, LANES), BF16),
        jax.ShapeDtypeStruct((n, LANES), BF16),
    ]
    out_specs = [row_spec(s.shape[1]) for s in out_shape]
    aliases = {}
    cache_mode = None
    if cache_stacks is not None:
        assert tile % seq == 0
        cache_mode = "init" if layer == 0 else "update"
        for stack in cache_stacks:
            depth, width = stack.shape[1], stack.shape[3]
            out_shape.append(jax.ShapeDtypeStruct(stack.shape, stack.dtype))
            if cache_mode == "init":
                out_specs.append(pl.BlockSpec((tile // seq, depth, seq, width), lambda i: (i, 0, 0, 0)))
            else:
                aliases[len(args)] = len(out_shape) - 1
                in_specs.append(pl.BlockSpec(memory_space=pl.ANY))
                args.append(stack)
                out_specs.append(pl.BlockSpec((tile // seq, None, seq, width), lambda i: (i, layer, 0, 0)))
    return pl.pallas_call(
        functools.partial(_front_kernel, rope=rope, cache_mode=cache_mode, sub=sub),
        out_shape=out_shape,
        grid=(n // tile,),
        in_specs=in_specs,
        out_specs=out_specs,
        input_output_aliases=aliases,
        compiler_params=pltpu.CompilerParams(
            dimension_semantics=("arbitrary",), vmem_limit_bytes=VMEM_LIMIT),
        name="front_lat" if rope else "front_ctx",
    )(*args)


def _scores(q, pieces):
    scores = []
    for k, _, mask in pieces:
        s = _dot_nt(q, k)
        if mask is not None:
            s = jnp.where(mask, s, NEG_INF)
        scores.append(s)
    return scores


def _softmax(scores, sink):
    m = functools.reduce(jnp.maximum, [jnp.max(s, axis=-1, keepdims=True) for s in scores])
    if sink is not None:
        m = jnp.maximum(m, sink)
    denom = jnp.exp2(sink - m) if sink is not None else jnp.zeros_like(m)
    probs = []
    for s in scores:
        p = jnp.exp2(s - m)
        denom = denom + jnp.sum(p, axis=-1, keepdims=True)
        probs.append(p.astype(BF16))
    return probs, 1.0 / denom


def _pv(probs, inv_denom, pieces):
    out = None
    for p, (_, v, _) in zip(probs, pieces):
        pv = _dot(p, v)
        out = pv if out is None else out + pv
    return out * inv_denom


def _attn_kernel(*refs, latent, tq, sub, seq, layer):
    sink_ref, qm_ref, kcat_ref, vm_ref, qs_ref, ks_ref, vs_ref = refs[:7]
    pos = 7
    if latent:
        cckv_ref, ckpe_ref, cks_ref, cvs_ref, wkv_ref, place_ref = refs[pos:pos + 6]
        pos += 6
    ym_ref, ys_ref = refs[pos:pos + 2]
    pos += 2
    if latent:
        kc_scr, vc_scr, ksc_scr, vsc_scr = refs[pos:pos + 4]
        qi = pl.program_id(1)

        @pl.when(qi == 0)
        def _expand_cache():
            kv = _dot(cckv_ref[...].astype(BF16), wkv_ref[...])
            kp = _dot(ckpe_ref[...].astype(BF16), place_ref[...])
            for hh in range(MLA_HEADS):
                kc_scr[:, hh * LANES:(hh + 1) * LANES] = (kv[:, hh * LANES:(hh + 1) * LANES] + kp).astype(BF16)
            vc_scr[...] = kv[:, MLA_HEADS * LANES:].astype(BF16)
            ksc_scr[...] = cks_ref[...].astype(BF16)
            vsc_scr[...] = cvs_ref[...].astype(BF16)

    tasks = []
    kv_of_lane = lax.broadcasted_iota(jnp.int32, (sub, LANES), 1) // HEAD_DIM
    for sb in range(tq // sub):
        qrows = slice(sb * sub, (sb + 1) * sub)
        krows = slice(None) if latent else qrows
        for hh in range(MLA_HEADS):
            cols = slice(hh * LANES, (hh + 1) * LANES)
            pieces = [(kcat_ref[krows, cols], vm_ref[krows, :], None)]
            if latent:
                pieces.append((kc_scr[:, cols], vc_scr[...], None))
            tasks.append((qm_ref[qrows, cols], pieces, None))
        if latent:
            win = sub + 2 * SWA_WINDOW
            q0 = qi * tq + sb * sub
            start = pl.multiple_of(jnp.clip(q0 - SWA_WINDOW, 0, seq - win), SWA_WINDOW)
            q_pos = q0 + lax.broadcasted_iota(jnp.int32, (sub, win), 0)
            k_pos = start + lax.broadcasted_iota(jnp.int32, (sub, win), 1)
            band_ok = jnp.abs(q_pos - k_pos) <= SWA_WINDOW
            swa_pieces = [(ks_ref[pl.ds(start, win), :], vs_ref[pl.ds(start, win), :], band_ok),
                          (ksc_scr[...], vsc_scr[...], None)]
        else:
            swa_pieces = [(ks_ref[krows, :], vs_ref[krows, :], None)]
        for g in range(SWA_GROUP):
            qg = qs_ref[qrows, g * LANES:(g + 1) * LANES].astype(F32)
            for n in range(SWA_KV_HEADS):
                qh = jnp.where(kv_of_lane == n, qg, 0.0).astype(BF16)
                tasks.append((qh, swa_pieces, sink_ref[layer, n * SWA_GROUP + g] * LOG2E))

    per_block = MLA_HEADS + SWA_GROUP * SWA_KV_HEADS
    head_of_lane = lax.broadcasted_iota(jnp.int32, (sub, MLA_HEADS * MLA_V), 1) // MLA_V
    om = jnp.zeros((sub, MLA_HEADS * MLA_V), F32)
    blk = jnp.zeros((sub, LANES), F32)
    scores = _scores(tasks[0][0], tasks[0][1])
    for i, (_, pieces, sink) in enumerate(tasks):
        probs, inv_denom = _softmax(scores, sink)
        scores = _scores(tasks[i + 1][0], tasks[i + 1][1]) if i + 1 < len(tasks) else None
        o = _pv(probs, inv_denom, pieces)
        sb, j = divmod(i, per_block)
        qrows = slice(sb * sub, (sb + 1) * sub)
        if j < MLA_HEADS:
            om = jnp.where(head_of_lane == j, o, om)
            if j == MLA_HEADS - 1:
                ym_ref[qrows, :] = om.astype(BF16)
        else:
            g, n = divmod(j - MLA_HEADS, SWA_KV_HEADS)
            blk = jnp.where(kv_of_lane == n, o, blk)
            if n == SWA_KV_HEADS - 1:
                ys_ref[qrows, g * LANES:(g + 1) * LANES] = blk.astype(BF16)


def _attn_call(front, sink, pw, cache, *, batch, seq, tq, sub, layer):
    qm, kcat, vm, qs, ks, vs = front
    latent = cache is not None
    assert tq % sub == 0 and (seq % tq == 0 if latent else (sub == seq and (batch * seq) % tq == 0))
    n_q = seq // tq if latent else 1
    kv_rows = seq if latent else tq
    grid = (batch, n_q) if latent else (batch * seq // tq, 1)

    def q_spec(width):
        return pl.BlockSpec((tq, width), lambda b, i: (b * n_q + i, 0))

    def seq_spec(width):
        return pl.BlockSpec((kv_rows, width), lambda b, i: (b, 0))

    in_specs = [
        pl.BlockSpec(memory_space=pltpu.SMEM),
        q_spec(MLA_HEADS * LANES), seq_spec(MLA_HEADS * LANES), seq_spec(MLA_HEADS * MLA_V),
        q_spec(2 * LANES), seq_spec(LANES), seq_spec(LANES),
    ]
    args = [sink, qm, kcat, vm, qs, ks, vs]
    scratch = []
    if latent:
        c_ckv, c_kpe, c_k, c_v = cache
        past = c_ckv.shape[2]

        def cache_spec(width):
            return pl.BlockSpec((None, None, past, width), lambda b, i: (b, layer, 0, 0))

        in_specs += [cache_spec(MLA_KV_LORA), cache_spec(MLA_ROPE), cache_spec(LANES), cache_spec(LANES),
                     _layer_spec((MLA_KV_LORA, KV_UP_COLS), layer),
                     _const_spec((MLA_ROPE, LANES))]
        args += [c_ckv, c_kpe, c_k, c_v, pw["w_kv_up"], pw["kpe_place"]]
        scratch = [pltpu.VMEM((past, MLA_HEADS * LANES), BF16), pltpu.VMEM((past, MLA_HEADS * MLA_V), BF16),
                   pltpu.VMEM((past, LANES), BF16), pltpu.VMEM((past, LANES), BF16)]
    n = batch * seq
    return pl.pallas_call(
        functools.partial(_attn_kernel, latent=latent, tq=tq, sub=sub, seq=seq, layer=layer),
        out_shape=[jax.ShapeDtypeStruct((n, MLA_HEADS * MLA_V), BF16),
                   jax.ShapeDtypeStruct((n, 2 * LANES), BF16)],
        grid=grid,
        in_specs=in_specs,
        out_specs=[q_spec(MLA_HEADS * MLA_V), q_spec(2 * LANES)],
        scratch_shapes=scratch,
        compiler_params=pltpu.CompilerParams(
            dimension_semantics=("arbitrary", "arbitrary"), vmem_limit_bytes=VMEM_LIMIT),
        name="attn_lat" if latent else "attn_ctx",
    )(*args)


def _back_kernel(x_ref, mod_ref, ab_ref, pc_ref, pprev_ref, pnext_ref, ysgu_ref, ym_ref, ys_ref, convw_ref,
                 wout_ref, n2_ref, w1_ref, w2_ref, fn_ref, o_ref, *, seq, sub, final, hidden_chunk):
    t = x_ref.shape[0]
    row = pl.program_id(0) * t + lax.broadcasted_iota(jnp.int32, (t, GROUP_W), 0)
    local = lax.broadcasted_iota(jnp.int32, (t, GROUP_W), 0)
    pc = pc_ref[...]
    prev_row = pprev_ref[SUBLANES - 1:SUBLANES, :]
    next_row = pnext_ref[0:1, :]
    up = jnp.where(local == 0, prev_row, pltpu.roll(pc, 1, 0))
    up = jnp.where(row % seq == 0, 0.0, up)
    dn = jnp.where(local == t - 1, next_row, pltpu.roll(pc, t - 1, 0))
    dn = jnp.where((row + 1) % seq == 0, 0.0, dn)
    conv = convw_ref[0:1, :] * up + convw_ref[1:2, :] * pc + convw_ref[2:3, :] * dn
    y_conv = (ab_ref[...] * conv).astype(BF16)

    y = jnp.concatenate([y_conv, ysgu_ref[...], ym_ref[...], ys_ref[...]], axis=1)
    gate1 = mod_ref[:, 2 * D_MODEL:3 * D_MODEL]
    shift2 = mod_ref[:, 3 * D_MODEL:4 * D_MODEL]
    scale2 = mod_ref[:, 4 * D_MODEL:5 * D_MODEL]
    gate2 = mod_ref[:, 5 * D_MODEL:6 * D_MODEL]
    subs = [slice(k * sub, (k + 1) * sub) for k in range(t // sub)]
    x1s = [x_ref[r, :] + gate1 * _dot(y[r], wout_ref[...]) for r in subs]
    for r, x1 in zip(subs, x1s):
        h2 = (_rms(x1, n2_ref[...]) * (1.0 + scale2) + shift2).astype(BF16)
        acc = jnp.zeros((sub, D_MODEL), F32)
        for c in range(MLP_HIDDEN // hidden_chunk):
            cols = slice(c * hidden_chunk, (c + 1) * hidden_chunk)
            hid = jnp.maximum(_dot(h2, w1_ref[:, cols]), 0.0)
            acc = acc + _dot((hid * hid).astype(BF16), w2_ref[cols, :])
        x2 = x1 + gate2 * acc
        if final:
            x2 = _rms(x2, fn_ref[...])
        o_ref[r, :] = x2


def _back_call(x, mod, ab, pc, ysgu, ym, ys, pw, final_norm, *, layer, seq, tile, sub, mod_row0, per_seq_mod,
               final):
    n = x.shape[0]
    halo_blocks = n // SUBLANES
    per_tile = tile // SUBLANES

    def row_spec(width):
        return pl.BlockSpec((tile, width), lambda i: (i, 0))

    def mod_row(i):
        return mod_row0 + (i * tile) // seq if per_seq_mod else mod_row0

    in_specs = [
        row_spec(D_MODEL),
        _mod_spec(N_MOD * D_MODEL, 0, layer, mod_row),
        row_spec(GROUP_W), row_spec(GROUP_W),
        pl.BlockSpec((SUBLANES, GROUP_W), lambda i: (jnp.maximum(i * per_tile - 1, 0), 0)),
        pl.BlockSpec((SUBLANES, GROUP_W), lambda i: (jnp.minimum((i + 1) * per_tile, halo_blocks - 1), 0)),
        row_spec(GROUP_W), row_spec(GROUP_W), row_spec(GROUP_W),
        _layer_spec((CONV_WIDTH, GROUP_W), layer),
        _layer_spec((D_MODEL, D_MODEL), layer),
        _layer_spec((1, D_MODEL), layer),
        _layer_spec((D_MODEL, MLP_HIDDEN), layer),
        _layer_spec((MLP_HIDDEN, D_MODEL), layer),
        _const_spec((1, D_MODEL)),
    ]
    return pl.pallas_call(
        functools.partial(_back_kernel, seq=seq, sub=sub, final=final, hidden_chunk=HIDDEN_CHUNK),
        out_shape=jax.ShapeDtypeStruct((n, D_MODEL), F32),
        grid=(n // tile,),
        in_specs=in_specs,
        out_specs=row_spec(D_MODEL),
        compiler_params=pltpu.CompilerParams(
            dimension_semantics=("arbitrary",), vmem_limit_bytes=VMEM_LIMIT),
        name="back",
    )(x, mod, ab, pc, pc, pc, ysgu, ym, ys, pw["conv_w"], pw["w_out"], pw["norm2"], pw["w1"], pw["w2"],
      final_norm)


def _rope_tables(n_tokens, rot_dim):
    rows = n_tokens // GRID_W
    row = jnp.repeat(jnp.arange(rows, dtype=F32), GRID_W)
    col = jnp.tile(jnp.arange(GRID_W, dtype=F32), rows)
    half = rot_dim // 2
    inv_freq = ROPE_THETA ** (-jnp.arange(0, half, 2, dtype=F32) / half)
    ang_r = row[:, None] * inv_freq[None, :]
    ang_c = col[:, None] * inv_freq[None, :]
    ang = jnp.concatenate([ang_r, ang_r, ang_c, ang_c], axis=-1)
    sign = jnp.where((jnp.arange(rot_dim) & (rot_dim // 4)) == 0, -1.0, 1.0).astype(F32)
    return jnp.cos(ang), jnp.sin(ang) * sign[None, :]


def _lane_tables(n_tokens):
    cos_s, sin_s = _rope_tables(n_tokens, HEAD_DIM)
    cos_s = jnp.tile(cos_s, (1, LANES // HEAD_DIM))
    sin_s = jnp.tile(sin_s, (1, LANES // HEAD_DIM))
    cos_m, sin_m = _rope_tables(n_tokens, MLA_ROPE)
    pad = LANES - MLA_ROPE
    cos_m = jnp.concatenate([cos_m, jnp.ones((n_tokens, pad), F32)], axis=1)
    sin_m = jnp.concatenate([sin_m, jnp.zeros((n_tokens, pad), F32)], axis=1)
    return cos_s, sin_s, cos_m, sin_m


def _prep_weights(norm1, norm2, w_in, conv_w, sgu_norm, sgu_w, sgu_b, mla_q_norm, mla_w_q_up, mla_kv_norm,
                  mla_w_kv_up, w_out, mlp_w1, mlp_w2):
    depth, d, _ = w_in.shape
    s_kpe = IN_MAIN + MLA_KV_LORA
    s_q = s_kpe + MLA_ROPE
    w_in_main = w_in[:, :, :IN_MAIN].astype(BF16)
    sq = w_in[:, :, s_q:s_q + 4 * HEAD_DIM].reshape(depth, d, SWA_KV_HEADS, SWA_GROUP, HEAD_DIM)
    sq = sq.transpose(0, 1, 3, 2, 4).reshape(depth, d, 4 * HEAD_DIM)
    w_in_tail = jnp.concatenate(
        [w_in[:, :, IN_MAIN:s_q], jnp.zeros((depth, d, LANES - MLA_ROPE), F32), sq,
         w_in[:, :, s_q + 4 * HEAD_DIM:]], axis=2).astype(BF16)

    wq = mla_w_q_up.reshape(depth, MLA_Q_LORA, MLA_HEADS, MLA_NOPE + MLA_ROPE)
    wq_p = jnp.concatenate(
        [wq[..., MLA_NOPE:], wq[..., :MLA_NOPE],
         jnp.zeros((depth, MLA_Q_LORA, MLA_HEADS, LANES - MLA_NOPE - MLA_ROPE), F32)], axis=-1)
    wq_p = wq_p.reshape(depth, MLA_Q_LORA, MLA_HEADS * LANES).astype(BF16)

    wkv = mla_w_kv_up.reshape(depth, MLA_KV_LORA, MLA_HEADS, MLA_NOPE + MLA_V)
    wk_p = jnp.concatenate(
        [jnp.zeros((depth, MLA_KV_LORA, MLA_HEADS, MLA_ROPE), F32), wkv[..., :MLA_NOPE],
         jnp.zeros((depth, MLA_KV_LORA, MLA_HEADS, LANES - MLA_NOPE - MLA_ROPE), F32)], axis=-1)
    wkv_p = jnp.concatenate(
        [wk_p.reshape(depth, MLA_KV_LORA, MLA_HEADS * LANES),
         wkv[..., MLA_NOPE:].reshape(depth, MLA_KV_LORA, MLA_HEADS * MLA_V)], axis=2).astype(BF16)

    sguw_p = sgu_w.transpose(0, 2, 1, 3).reshape(depth, CHUNK, SGU_HEADS * CHUNK).astype(BF16)
    sgub_p = jnp.repeat(sgu_b.transpose(0, 2, 1), HEAD_DIM, axis=2)

    wo_swa = w_out[:, 3 * GROUP_W:].reshape(depth, SWA_KV_HEADS, SWA_GROUP, HEAD_DIM, D_MODEL)
    wo_swa = wo_swa.transpose(0, 2, 1, 3, 4).reshape(depth, GROUP_W, D_MODEL)
    wo_p = jnp.concatenate([w_out[:, :3 * GROUP_W], wo_swa], axis=1).astype(BF16)

    return {
        "norm1": norm1[:, None, :], "norm2": norm2[:, None, :], "w_in": w_in_main, "w_in_tail": w_in_tail,
        "conv_w": conv_w,
        "sgu_norm": sgu_norm[:, None, :], "sgu_w": sguw_p, "sgu_b": sgub_p,
        "q_norm": mla_q_norm[:, None, :], "w_q_up": wq_p, "kv_norm": mla_kv_norm[:, None, :],
        "w_kv_up": wkv_p, "kpe_place": jnp.eye(MLA_ROPE, LANES, dtype=BF16), "w_out": wo_p,
        "w1": mlp_w1.astype(BF16), "w2": mlp_w2.astype(BF16),
    }


def _pick_tile(n, pref):
    while n % pref:
        pref //= 2
    return pref


def kernel(x_prompt, x_sample, cache_mla_ckv, cache_mla_kpe, cache_swa_k, cache_swa_v, c, c_ctx, w_ada, b_ada,
           norm1, norm2, w_in, conv_w, sgu_norm, sgu_w, sgu_b, mla_q_norm, mla_w_q_up, mla_kv_norm, mla_w_kv_up,
           swa_sink, w_out, mlp_w1, mlp_w2, final_norm):
    batch, seq, d = x_prompt.shape
    dec_batch, dec_seq, _ = x_sample.shape
    depth = w_ada.shape[0]
    past = cache_mla_ckv.shape[2]
    assert d == D_MODEL and seq % CHUNK == 0 and dec_seq % (2 * CHUNK) == 0 and 1 + dec_batch <= MOD_ROWS

    c_all = jnp.concatenate(
        [c_ctx[None, :], c, jnp.zeros((MOD_ROWS - 1 - dec_batch, d), F32)], axis=0)
    mod = _ada_call(c_all, w_ada, b_ada).reshape(depth, MOD_ROWS, 1, N_MOD * d)

    pw = _prep_weights(norm1, norm2, w_in, conv_w, sgu_norm, sgu_w, sgu_b, mla_q_norm, mla_w_q_up, mla_kv_norm,
                       mla_w_kv_up, w_out, mlp_w1, mlp_w2)
    tables = _lane_tables(dec_seq)
    cache = (cache_mla_ckv, cache_mla_kpe,
             cache_swa_k.reshape(dec_batch, depth, past, LANES), cache_swa_v.reshape(dec_batch, depth, past, LANES))
    fn = final_norm[None, :]

    xp = x_prompt.reshape(batch * seq, d)
    xs = x_sample.reshape(dec_batch * dec_seq, d)
    tile_p = _pick_tile(batch * seq, ROW_SUB)
    tile_s = _pick_tile(dec_seq, ROW_SUB)
    tq_s = _pick_tile(dec_seq, QUERY_SUB)
    ftile_p = _pick_tile(batch * seq, 2 * tile_p)
    ftile_s = _pick_tile(dec_seq, 2 * tile_s)
    atile_s = _pick_tile(dec_seq, 2 * tq_s)
    atile_p = _pick_tile(batch * seq, CTX_SEQS_PER_STEP * seq)
    bsub = _pick_tile(min(ftile_p, ftile_s), BACK_SUB)
    new_cache = [jax.ShapeDtypeStruct((batch, depth, seq, w), F32) for w in (MLA_KV_LORA, MLA_ROPE, LANES, LANES)]
    for l in range(depth):
        final = l == depth - 1

        fo = _front_call(xp, mod, pw, None, new_cache, layer=l, seq=seq, tile=ftile_p, sub=tile_p, mod_row0=0)
        ab, pc, ysgu = fo[:3]
        new_cache = list(fo[9:13])
        ym, ys = _attn_call(fo[3:9], swa_sink, pw, None, batch=batch, seq=seq, tq=atile_p, sub=seq, layer=l)
        xp = _back_call(xp, mod, ab, pc, ysgu, ym, ys, pw, fn, layer=l, seq=seq, tile=ftile_p, sub=bsub,
                        mod_row0=0, per_seq_mod=False, final=final)

        fo = _front_call(xs, mod, pw, tables, None, layer=l, seq=dec_seq, tile=ftile_s, sub=tile_s, mod_row0=1)
        ab, pc, ysgu = fo[:3]
        ym, ys = _attn_call(fo[3:9], swa_sink, pw, cache, batch=dec_batch, seq=dec_seq, tq=atile_s, sub=tq_s,
                            layer=l)
        xs = _back_call(xs, mod, ab, pc, ysgu, ym, ys, pw, fn, layer=l, seq=dec_seq, tile=ftile_s, sub=bsub,
                        mod_row0=1, per_seq_mod=True, final=final)

    swa_shape = (batch, depth, seq, SWA_KV_HEADS, HEAD_DIM)
    return (xp.reshape(batch, seq, d), xs.reshape(dec_batch, dec_seq, d), new_cache[0], new_cache[1],
            new_cache[2].reshape(swa_shape), new_cache[3].reshape(swa_shape))
```

```python
import functools

import jax
import jax.numpy as jnp
from jax import lax
from jax.experimental import pallas as pl
from jax.experimental.pallas import tpu as pltpu

D_MODEL = 1024
GROUP_W = 256
HEAD_DIM = 64
CONV_WIDTH = 3
CHUNK = 128
SGU_HEADS = 4
MLA_HEADS = 4
MLA_NOPE = 64
MLA_ROPE = 32
MLA_V = 64
MLA_Q_LORA = 256
MLA_KV_LORA = 128
SWA_KV_HEADS = 2
SWA_GROUP = 2
SWA_WINDOW = 128
MLP_HIDDEN = 4096
GRID_W = 64
ROPE_THETA = 10000.0
EPS = 1e-6
N_MOD = 6
MLA_SCALE = (MLA_NOPE + MLA_ROPE) ** -0.5
SWA_SCALE = HEAD_DIM ** -0.5
NEG_INF = -1e30
LOG2E = 1.4426950408889634

LANES = 128
SUBLANES = 8
MOD_ROWS = 16
VMEM_LIMIT = 56 * 1024 * 1024
ROW_SUB = 512
QUERY_SUB = 256
BACK_SUB = 256
CTX_SEQS_PER_STEP = 4
ADA_COLS = 1536
HIDDEN_CHUNK = 1024

C_CONV = 0
C_SGU = 768
C_CQ = 1280
IN_MAIN = 1536
T_KV = 0
T_SWA = 256
IN_TAIL = 768
KV_UP_COLS = MLA_HEADS * LANES + MLA_HEADS * MLA_V

BF16 = jnp.bfloat16
F32 = jnp.float32


def _dot(a, b):
    return jnp.dot(a, b, preferred_element_type=F32)


def _dot_nt(a, b):
    return lax.dot_general(a, b, (((1,), (1,)), ((), ())), preferred_element_type=F32)


def _rms(x, g):
    return x * lax.rsqrt(jnp.mean(x * x, axis=-1, keepdims=True) + EPS) * g


def _rope(x, cos, sin_signed, quarter):
    lane = lax.broadcasted_iota(jnp.int32, x.shape, 1)
    first = (lane & quarter) == 0
    rot = jnp.where(first, pltpu.roll(x, LANES - quarter, 1), pltpu.roll(x, quarter, 1))
    return x * cos + rot * sin_signed


def _layer_spec(block, layer):
    nd = len(block)
    return pl.BlockSpec((None,) + tuple(block), lambda *_: (layer,) + (0,) * nd, pipeline_mode=pl.Buffered(1))


def _const_spec(shape):
    nd = len(shape)
    return pl.BlockSpec(shape, lambda *_: (0,) * nd, pipeline_mode=pl.Buffered(1))


def _mod_spec(width, col_block, layer, row_of_step):
    return pl.BlockSpec((None, None, 1, width), lambda *idx: (layer, row_of_step(*idx), 0, col_block))


def _ada_kernel(c_ref, w_ref, b_ref, o_ref):
    c = c_ref[...]
    s = (c * jax.nn.sigmoid(c)).astype(BF16)
    o_ref[...] = _dot(s, w_ref[...].astype(BF16)) + b_ref[...]


def _ada_call(c_all, w_ada, b_ada):
    depth = w_ada.shape[0]
    tn = ADA_COLS
    n_out = N_MOD * D_MODEL
    return pl.pallas_call(
        _ada_kernel,
        out_shape=jax.ShapeDtypeStruct((depth, MOD_ROWS, n_out), F32),
        grid=(depth, n_out // tn),
        in_specs=[
            pl.BlockSpec((MOD_ROWS, D_MODEL), lambda l, j: (0, 0)),
            pl.BlockSpec((None, D_MODEL, tn), lambda l, j: (l, 0, j)),
            pl.BlockSpec((None, 1, tn), lambda l, j: (l, 0, j)),
        ],
        out_specs=pl.BlockSpec((None, MOD_ROWS, tn), lambda l, j: (l, 0, j)),
        compiler_params=pltpu.CompilerParams(
            dimension_semantics=("arbitrary", "arbitrary"), vmem_limit_bytes=VMEM_LIMIT),
        name="ada_mod",
    )(c_all, w_ada, b_ada.reshape(depth, 1, n_out))


def _front_kernel(*refs, rope, cache_mode, sub):
    (x_ref, mod_ref, n1_ref, win_ref, wtail_ref, sgun_ref, sguw_ref, sgub_ref, qn_ref, wq_ref, kvn_ref,
     wkv_ref) = refs[:12]
    pos = 12
    if rope:
        cos_s_ref, sin_s_ref, cos_m_ref, sin_m_ref = refs[pos:pos + 4]
        pos += 4
    cache_out = cache_mode is not None
    if cache_mode == "update":
        pos += 4
    (ab_ref, pc_ref, ysgu_ref, qm_ref, kcat_ref, vm_ref, qs_ref, ks_ref, vs_ref) = refs[pos:pos + 9]
    pos += 9
    if cache_out:
        ckv_o, kpe_o, ksw_o, vsw_o = refs[pos:pos + 4]
        if cache_mode == "init":
            for o_ref in (ckv_o, kpe_o, ksw_o, vsw_o):
                o_ref[:, 1:] = jnp.zeros((o_ref.shape[0], o_ref.shape[1] - 1) + o_ref.shape[2:], F32)

        def put(o_ref, val, st):
            per = val.shape[0] // o_ref.shape[-2]
            slab = val.reshape((per,) + o_ref.shape[-2:])
            if cache_mode == "init":
                o_ref[st * per:(st + 1) * per, 0] = slab
            else:
                o_ref[st * per:(st + 1) * per] = slab

        def put_t(o_ref, val, st):
            width, seq_len = o_ref.shape[-2:]
            per = val.shape[0] // seq_len
            val_t = val.T
            for s in range(per):
                slab = val_t[:width, s * seq_len:(s + 1) * seq_len]
                if cache_mode == "init":
                    o_ref[st * per + s, 0] = slab
                else:
                    o_ref[st * per + s] = slab

    shift = mod_ref[:, 0:D_MODEL]
    scale = mod_ref[:, D_MODEL:2 * D_MODEL]
    head_of_lane = lax.broadcasted_iota(jnp.int32, (CHUNK, GROUP_W), 1) // HEAD_DIM

    for st in range(x_ref.shape[0] // sub):
        rows = slice(st * sub, (st + 1) * sub)
        h = (_rms(x_ref[rows, :], n1_ref[...]) * (1.0 + scale) + shift).astype(BF16)
        if rope:
            cos_s, sin_s = cos_s_ref[rows, :], sin_s_ref[rows, :]
            cos_m, sin_m = cos_m_ref[rows, :], sin_m_ref[rows, :]

        zq = _dot(h, win_ref[:, C_CQ:C_CQ + MLA_Q_LORA])
        zkv = _dot(h, wtail_ref[:, T_KV:T_KV + 2 * LANES])
        zs = _dot(h, win_ref[:, C_SGU:C_SGU + 2 * GROUP_W])

        cqn = _rms(zq, qn_ref[...]).astype(BF16)
        q = _dot(cqn, wq_ref[...]) * (MLA_SCALE * LOG2E)
        for hh in range(MLA_HEADS):
            qh = q[:, hh * LANES:(hh + 1) * LANES]
            if rope:
                qh = _rope(qh, cos_m, sin_m, MLA_ROPE // 4)
            qm_ref[rows, hh * LANES:(hh + 1) * LANES] = qh.astype(BF16)

        ckv = _rms(zkv[:, :MLA_KV_LORA], kvn_ref[...])
        kpe = zkv[:, MLA_KV_LORA:]
        if cache_out:
            put(ckv_o, ckv, st)
            put_t(kpe_o, kpe, st)
        if rope:
            kpe = _rope(kpe, cos_m, sin_m, MLA_ROPE // 4)
        kv = _dot(ckv.astype(BF16), wkv_ref[...])
        for hh in range(MLA_HEADS):
            kcat_ref[rows, hh * LANES:(hh + 1) * LANES] = (kv[:, hh * LANES:(hh + 1) * LANES] + kpe).astype(BF16)
        vm_ref[rows, :] = kv[:, MLA_HEADS * LANES:].astype(BF16)

        zc = _dot(h, win_ref[:, C_CONV:C_CONV + 3 * GROUP_W])
        ab_ref[rows, :] = zc[:, :GROUP_W]
        pc_ref[rows, :] = zc[:, GROUP_W:2 * GROUP_W] * zc[:, 2 * GROUP_W:]

        zw = _dot(h, wtail_ref[:, T_SWA:T_SWA + 4 * LANES])
        sk = zw[:, 2 * LANES:3 * LANES]
        sv_ = zw[:, 3 * LANES:]
        if cache_out:
            put_t(ksw_o, sk, st)
            put_t(vsw_o, sv_, st)
        for g in range(SWA_GROUP):
            qg = zw[:, g * LANES:(g + 1) * LANES] * (SWA_SCALE * LOG2E)
            if rope:
                qg = _rope(qg, cos_s, sin_s, HEAD_DIM // 4)
            qs_ref[rows, g * LANES:(g + 1) * LANES] = qg.astype(BF16)
        if rope:
            sk = _rope(sk, cos_s, sin_s, HEAD_DIM // 4)
        ks_ref[rows, :] = sk.astype(BF16)
        vs_ref[rows, :] = sv_.astype(BF16)

        gz = jax.nn.gelu(zs, approximate=True)
        u = gz[:, :GROUP_W]
        vn = _rms(gz[:, GROUP_W:], sgun_ref[...])
        for c in range(sub // CHUNK):
            crow = slice(c * CHUNK, (c + 1) * CHUNK)
            vc = vn[crow]
            stacked = jnp.concatenate(
                [jnp.where(head_of_lane == hh, vc, 0.0).astype(BF16) for hh in range(SGU_HEADS)], axis=0)
            sv = _dot(sguw_ref[...], stacked) + sgub_ref[...]
            ysgu_ref[st * sub + c * CHUNK:st * sub + (c + 1) * CHUNK, :] = (u[crow] * sv).astype(BF16)


def _front_call(x, mod, pw, tables, cache_stacks, *, layer, seq, tile, sub, mod_row0):
    n = x.shape[0]
    rope = tables is not None
    tiles_per_seq = max(seq // tile, 1)

    def row_spec(width):
        return pl.BlockSpec((tile, width), lambda i: (i, 0))

    def mod_row(i):
        return mod_row0 + (i * tile) // seq if rope else mod_row0

    in_specs = [
        row_spec(D_MODEL),
        _mod_spec(2 * D_MODEL, 0, layer, mod_row),
        _layer_spec((1, D_MODEL), layer),
        _layer_spec((D_MODEL, IN_MAIN), layer),
        _layer_spec((D_MODEL, IN_TAIL), layer),
        _layer_spec((1, GROUP_W), layer),
        _layer_spec((CHUNK, SGU_HEADS * CHUNK), layer),
        _layer_spec((CHUNK, GROUP_W), layer),
        _layer_spec((1, MLA_Q_LORA), layer),
        _layer_spec((MLA_Q_LORA, MLA_HEADS * LANES), layer),
        _layer_spec((1, MLA_KV_LORA), layer),
        _layer_spec((MLA_KV_LORA, KV_UP_COLS), layer),
    ]
    args = [x, mod, pw["norm1"], pw["w_in"], pw["w_in_tail"], pw["sgu_norm"], pw["sgu_w"], pw["sgu_b"],
            pw["q_norm"], pw["w_q_up"], pw["kv_norm"], pw["w_kv_up"]]
    if rope:
        in_specs += [pl.BlockSpec((tile, LANES), lambda i: (i % tiles_per_seq, 0))] * 4
        args += list(tables)

    out_shape = [
        jax.ShapeDtypeStruct((n, GROUP_W), F32),
        jax.ShapeDtypeStruct((n, GROUP_W), F32),
        jax.ShapeDtypeStruct((n, GROUP_W), BF16),
        jax.ShapeDtypeStruct((n, MLA_HEADS * LANES), BF16),
        jax.ShapeDtypeStruct((n, MLA_HEADS * LANES), BF16),
        jax.ShapeDtypeStruct((n, MLA_HEADS * MLA_V), BF16),
        jax.ShapeDtypeStruct((n, 2 * LANES), BF16),
        jax.ShapeDtypeStruct((n, LANES), BF16),
        jax.ShapeDtypeStruct((n, LANES), BF16),
    ]
    out_specs = [row_spec(s.shape[1]) for s in out_shape]
    aliases = {}
    cache_mode = None
    if cache_stacks is not None:
        assert tile % seq == 0
        cache_mode = "init" if layer == 0 else "update"
        for stack in cache_stacks:
            depth, slab = stack.shape[1], tuple(stack.shape[2:])
            out_shape.append(jax.ShapeDtypeStruct(stack.shape, stack.dtype))
            if cache_mode == "init":
                out_specs.append(pl.BlockSpec((tile // seq, depth) + slab, lambda i: (i, 0, 0, 0)))
            else:
                aliases[len(args)] = len(out_shape) - 1
                in_specs.append(pl.BlockSpec(memory_space=pl.ANY))
                args.append(stack)
                out_specs.append(pl.BlockSpec((tile // seq, None) + slab, lambda i: (i, layer, 0, 0)))
    return pl.pallas_call(
        functools.partial(_front_kernel, rope=rope, cache_mode=cache_mode, sub=sub),
        out_shape=out_shape,
        grid=(n // tile,),
        in_specs=in_specs,
        out_specs=out_specs,
        input_output_aliases=aliases,
        compiler_params=pltpu.CompilerParams(
            dimension_semantics=("arbitrary",), vmem_limit_bytes=VMEM_LIMIT),
        name="front_lat" if rope else "front_ctx",
    )(*args)


def _scores(q, pieces):
    scores = []
    for k, _, mask, transposed in pieces:
        s = _dot(q, k) if transposed else _dot_nt(q, k)
        if mask is not None:
            s = jnp.where(mask, s, NEG_INF)
        scores.append(s)
    return scores


def _softmax(scores, sink):
    m = functools.reduce(jnp.maximum, [jnp.max(s, axis=-1, keepdims=True) for s in scores])
    if sink is not None:
        m = jnp.maximum(m, sink)
    denom = jnp.exp2(sink - m) if sink is not None else jnp.zeros_like(m)
    probs = []
    for s in scores:
        p = jnp.exp2(s - m)
        denom = denom + jnp.sum(p, axis=-1, keepdims=True)
        probs.append(p.astype(BF16))
    return probs, 1.0 / denom


def _pv(probs, inv_denom, pieces):
    out = None
    for p, (_, v, _, transposed) in zip(probs, pieces):
        pv = _dot_nt(p, v) if transposed else _dot(p, v)
        out = pv if out is None else out + pv
    return out * inv_denom


def _attn_kernel(*refs, latent, tq, sub, seq, layer):
    sink_ref, qm_ref, kcat_ref, vm_ref, qs_ref, ks_ref, vs_ref = refs[:7]
    pos = 7
    if latent:
        cckv_ref, ckpe_ref, cks_ref, cvs_ref, wkv_ref, place_ref = refs[pos:pos + 6]
        pos += 6
    ym_ref, ys_ref = refs[pos:pos + 2]
    pos += 2
    if latent:
        kc_scr, vc_scr, ksc_scr, vsc_scr = refs[pos:pos + 4]
        qi = pl.program_id(1)

        @pl.when(qi == 0)
        def _expand_cache():
            kv = _dot(cckv_ref[...].astype(BF16), wkv_ref[...])
            kp = _dot(place_ref[...], ckpe_ref[...]).T
            for hh in range(MLA_HEADS):
                kc_scr[:, hh * LANES:(hh + 1) * LANES] = (kv[:, hh * LANES:(hh + 1) * LANES] + kp).astype(BF16)
            vc_scr[...] = kv[:, MLA_HEADS * LANES:].astype(BF16)
            ksc_scr[...] = cks_ref[...].astype(BF16)
            vsc_scr[...] = cvs_ref[...].astype(BF16)

    tasks = []
    kv_of_lane = lax.broadcasted_iota(jnp.int32, (sub, LANES), 1) // HEAD_DIM
    for sb in range(tq // sub):
        qrows = slice(sb * sub, (sb + 1) * sub)
        krows = slice(None) if latent else qrows
        for hh in range(MLA_HEADS):
            cols = slice(hh * LANES, (hh + 1) * LANES)
            pieces = [(kcat_ref[krows, cols], vm_ref[krows, :], None, False)]
            if latent:
                pieces.append((kc_scr[:, cols], vc_scr[...], None, False))
            tasks.append((qm_ref[qrows, cols], pieces, None))
        if latent:
            win = sub + 2 * SWA_WINDOW
            q0 = qi * tq + sb * sub
            start = pl.multiple_of(jnp.clip(q0 - SWA_WINDOW, 0, seq - win), SWA_WINDOW)
            q_pos = q0 + lax.broadcasted_iota(jnp.int32, (sub, win), 0)
            k_pos = start + lax.broadcasted_iota(jnp.int32, (sub, win), 1)
            band_ok = jnp.abs(q_pos - k_pos) <= SWA_WINDOW
            swa_pieces = [(ks_ref[pl.ds(start, win), :], vs_ref[pl.ds(start, win), :], band_ok, False),
                          (ksc_scr[...], vsc_scr[...], None, True)]
        else:
            swa_pieces = [(ks_ref[krows, :], vs_ref[krows, :], None, False)]
        for g in range(SWA_GROUP):
            qg = qs_ref[qrows, g * LANES:(g + 1) * LANES].astype(F32)
            for n in range(SWA_KV_HEADS):
                qh = jnp.where(kv_of_lane == n, qg, 0.0).astype(BF16)
                tasks.append((qh, swa_pieces, sink_ref[layer, n * SWA_GROUP + g] * LOG2E))

    per_block = MLA_HEADS + SWA_GROUP * SWA_KV_HEADS
    head_of_lane = lax.broadcasted_iota(jnp.int32, (sub, MLA_HEADS * MLA_V), 1) // MLA_V
    om = jnp.zeros((sub, MLA_HEADS * MLA_V), F32)
    blk = jnp.zeros((sub, LANES), F32)
    scores = _scores(tasks[0][0], tasks[0][1])
    for i, (_, pieces, sink) in enumerate(tasks):
        probs, inv_denom = _softmax(scores, sink)
        scores = _scores(tasks[i + 1][0], tasks[i + 1][1]) if i + 1 < len(tasks) else None
        o = _pv(probs, inv_denom, pieces)
        sb, j = divmod(i, per_block)
        qrows = slice(sb * sub, (sb + 1) * sub)
        if j < MLA_HEADS:
            om = jnp.where(head_of_lane == j, o, om)
            if j == MLA_HEADS - 1:
                ym_ref[qrows, :] = om.astype(BF16)
        else:
            g, n = divmod(j - MLA_HEADS, SWA_KV_HEADS)
            blk = jnp.where(kv_of_lane == n, o, blk)
            if n == SWA_KV_HEADS - 1:
                ys_ref[qrows, g * LANES:(g + 1) * LANES] = blk.astype(BF16)


def _attn_call(front, sink, pw, cache, *, batch, seq, tq, sub, layer):
    qm, kcat, vm, qs, ks, vs = front
    latent = cache is not None
    assert tq % sub == 0 and (seq % tq == 0 if latent else (sub == seq and (batch * seq) % tq == 0))
    n_q = seq // tq if latent else 1
    kv_rows = seq if latent else tq
    grid = (batch, n_q) if latent else (batch * seq // tq, 1)

    def q_spec(width):
        return pl.BlockSpec((tq, width), lambda b, i: (b * n_q + i, 0))

    def seq_spec(width):
        return pl.BlockSpec((kv_rows, width), lambda b, i: (b, 0))

    in_specs = [
        pl.BlockSpec(memory_space=pltpu.SMEM),
        q_spec(MLA_HEADS * LANES), seq_spec(MLA_HEADS * LANES), seq_spec(MLA_HEADS * MLA_V),
        q_spec(2 * LANES), seq_spec(LANES), seq_spec(LANES),
    ]
    args = [sink, qm, kcat, vm, qs, ks, vs]
    scratch = []
    if latent:
        c_ckv, c_kpe, c_k, c_v = cache
        past = c_ckv.shape[2]

        def cache_spec(width):
            return pl.BlockSpec((None, None, past, width), lambda b, i: (b, layer, 0, 0))

        def cache_spec_t(width):
            return pl.BlockSpec((None, None, width, past), lambda b, i: (b, layer, 0, 0))

        in_specs += [cache_spec(MLA_KV_LORA), cache_spec_t(MLA_ROPE), cache_spec_t(LANES), cache_spec_t(LANES),
                     _layer_spec((MLA_KV_LORA, KV_UP_COLS), layer),
                     _const_spec((LANES, MLA_ROPE))]
        args += [c_ckv, c_kpe, c_k, c_v, pw["w_kv_up"], pw["kpe_place"]]
        scratch = [pltpu.VMEM((past, MLA_HEADS * LANES), BF16), pltpu.VMEM((past, MLA_HEADS * MLA_V), BF16),
                   pltpu.VMEM((LANES, past), BF16), pltpu.VMEM((LANES, past), BF16)]
    n = batch * seq
    return pl.pallas_call(
        functools.partial(_attn_kernel, latent=latent, tq=tq, sub=sub, seq=seq, layer=layer),
        out_shape=[jax.ShapeDtypeStruct((n, MLA_HEADS * MLA_V), BF16),
                   jax.ShapeDtypeStruct((n, 2 * LANES), BF16)],
        grid=grid,
        in_specs=in_specs,
        out_specs=[q_spec(MLA_HEADS * MLA_V), q_spec(2 * LANES)],
        scratch_shapes=scratch,
        compiler_params=pltpu.CompilerParams(
            dimension_semantics=("arbitrary", "arbitrary"), vmem_limit_bytes=VMEM_LIMIT),
        name="attn_lat" if latent else "attn_ctx",
    )(*args)


def _back_kernel(x_ref, mod_ref, ab_ref, pc_ref, pprev_ref, pnext_ref, ysgu_ref, ym_ref, ys_ref, convw_ref,
                 wout_ref, n2_ref, w1_ref, w2_ref, fn_ref, o_ref, *, seq, sub, final, hidden_chunk):
    t = x_ref.shape[0]
    row = pl.program_id(0) * t + lax.broadcasted_iota(jnp.int32, (t, GROUP_W), 0)
    local = lax.broadcasted_iota(jnp.int32, (t, GROUP_W), 0)
    pc = pc_ref[...]
    prev_row = pprev_ref[SUBLANES - 1:SUBLANES, :]
    next_row = pnext_ref[0:1, :]
    up = jnp.where(local == 0, prev_row, pltpu.roll(pc, 1, 0))
    up = jnp.where(row % seq == 0, 0.0, up)
    dn = jnp.where(local == t - 1, next_row, pltpu.roll(pc, t - 1, 0))
    dn = jnp.where((row + 1) % seq == 0, 0.0, dn)
    conv = convw_ref[0:1, :] * up + convw_ref[1:2, :] * pc + convw_ref[2:3, :] * dn
    y_conv = (ab_ref[...] * conv).astype(BF16)

    y = jnp.concatenate([y_conv, ysgu_ref[...], ym_ref[...], ys_ref[...]], axis=1)
    gate1 = mod_ref[:, 2 * D_MODEL:3 * D_MODEL]
    shift2 = mod_ref[:, 3 * D_MODEL:4 * D_MODEL]
    scale2 = mod_ref[:, 4 * D_MODEL:5 * D_MODEL]
    gate2 = mod_ref[:, 5 * D_MODEL:6 * D_MODEL]
    subs = [slice(k * sub, (k + 1) * sub) for k in range(t // sub)]
    x1s = [x_ref[r, :] + gate1 * _dot(y[r], wout_ref[...]) for r in subs]
    for r, x1 in zip(subs, x1s):
        h2 = (_rms(x1, n2_ref[...]) * (1.0 + scale2) + shift2).astype(BF16)
        acc = jnp.zeros((sub, D_MODEL), F32)
        for c in range(MLP_HIDDEN // hidden_chunk):
            cols = slice(c * hidden_chunk, (c + 1) * hidden_chunk)
            hid = jnp.maximum(_dot(h2, w1_ref[:, cols]), 0.0)
            acc = acc + _dot((hid * hid).astype(BF16), w2_ref[cols, :])
        x2 = x1 + gate2 * acc
        if final:
            x2 = _rms(x2, fn_ref[...])
        o_ref[r, :] = x2


def _back_call(x, mod, ab, pc, ysgu, ym, ys, pw, final_norm, *, layer, seq, tile, sub, mod_row0, per_seq_mod,
               final):
    n = x.shape[0]
    halo_blocks = n // SUBLANES
    per_tile = tile // SUBLANES

    def row_spec(width):
        return pl.BlockSpec((tile, width), lambda i: (i, 0))

    def mod_row(i):
        return mod_row0 + (i * tile) // seq if per_seq_mod else mod_row0

    in_specs = [
        row_spec(D_MODEL),
        _mod_spec(N_MOD * D_MODEL, 0, layer, mod_row),
        row_spec(GROUP_W), row_spec(GROUP_W),
        pl.BlockSpec((SUBLANES, GROUP_W), lambda i: (jnp.maximum(i * per_tile - 1, 0), 0)),
        pl.BlockSpec((SUBLANES, GROUP_W), lambda i: (jnp.minimum((i + 1) * per_tile, halo_blocks - 1), 0)),
        row_spec(GROUP_W), row_spec(GROUP_W), row_spec(GROUP_W),
        _layer_spec((CONV_WIDTH, GROUP_W), layer),
        _layer_spec((D_MODEL, D_MODEL), layer),
        _layer_spec((1, D_MODEL), layer),
        _layer_spec((D_MODEL, MLP_HIDDEN), layer),
        _layer_spec((MLP_HIDDEN, D_MODEL), layer),
        _const_spec((1, D_MODEL)),
    ]
    return pl.pallas_call(
        functools.partial(_back_kernel, seq=seq, sub=sub, final=final, hidden_chunk=HIDDEN_CHUNK),
        out_shape=jax.ShapeDtypeStruct((n, D_MODEL), F32),
        grid=(n // tile,),
        in_specs=in_specs,
        out_specs=row_spec(D_MODEL),
        compiler_params=pltpu.CompilerParams(
            dimension_semantics=("arbitrary",), vmem_limit_bytes=VMEM_LIMIT),
        name="back",
    )(x, mod, ab, pc, pc, pc, ysgu, ym, ys, pw["conv_w"], pw["w_out"], pw["norm2"], pw["w1"], pw["w2"],
      final_norm)


def _rope_tables(n_tokens, rot_dim):
    rows = n_tokens // GRID_W
    row = jnp.repeat(jnp.arange(rows, dtype=F32), GRID_W)
    col = jnp.tile(jnp.arange(GRID_W, dtype=F32), rows)
    half = rot_dim // 2
    inv_freq = ROPE_THETA ** (-jnp.arange(0, half, 2, dtype=F32) / half)
    ang_r = row[:, None] * inv_freq[None, :]
    ang_c = col[:, None] * inv_freq[None, :]
    ang = jnp.concatenate([ang_r, ang_r, ang_c, ang_c], axis=-1)
    sign = jnp.where((jnp.arange(rot_dim) & (rot_dim // 4)) == 0, -1.0, 1.0).astype(F32)
    return jnp.cos(ang), jnp.sin(ang) * sign[None, :]


def _lane_tables(n_tokens):
    cos_s, sin_s = _rope_tables(n_tokens, HEAD_DIM)
    cos_s = jnp.tile(cos_s, (1, LANES // HEAD_DIM))
    sin_s = jnp.tile(sin_s, (1, LANES // HEAD_DIM))
    cos_m, sin_m = _rope_tables(n_tokens, MLA_ROPE)
    pad = LANES - MLA_ROPE
    cos_m = jnp.concatenate([cos_m, jnp.ones((n_tokens, pad), F32)], axis=1)
    sin_m = jnp.concatenate([sin_m, jnp.zeros((n_tokens, pad), F32)], axis=1)
    return cos_s, sin_s, cos_m, sin_m


def _prep_weights(norm1, norm2, w_in, conv_w, sgu_norm, sgu_w, sgu_b, mla_q_norm, mla_w_q_up, mla_kv_norm,
                  mla_w_kv_up, w_out, mlp_w1, mlp_w2):
    depth, d, _ = w_in.shape
    s_kpe = IN_MAIN + MLA_KV_LORA
    s_q = s_kpe + MLA_ROPE
    w_in_main = w_in[:, :, :IN_MAIN].astype(BF16)
    sq = w_in[:, :, s_q:s_q + 4 * HEAD_DIM].reshape(depth, d, SWA_KV_HEADS, SWA_GROUP, HEAD_DIM)
    sq = sq.transpose(0, 1, 3, 2, 4).reshape(depth, d, 4 * HEAD_DIM)
    w_in_tail = jnp.concatenate(
        [w_in[:, :, IN_MAIN:s_q], jnp.zeros((depth, d, LANES - MLA_ROPE), F32), sq,
         w_in[:, :, s_q + 4 * HEAD_DIM:]], axis=2).astype(BF16)

    wq = mla_w_q_up.reshape(depth, MLA_Q_LORA, MLA_HEADS, MLA_NOPE + MLA_ROPE)
    wq_p = jnp.concatenate(
        [wq[..., MLA_NOPE:], wq[..., :MLA_NOPE],
         jnp.zeros((depth, MLA_Q_LORA, MLA_HEADS, LANES - MLA_NOPE - MLA_ROPE), F32)], axis=-1)
    wq_p = wq_p.reshape(depth, MLA_Q_LORA, MLA_HEADS * LANES).astype(BF16)

    wkv = mla_w_kv_up.reshape(depth, MLA_KV_LORA, MLA_HEADS, MLA_NOPE + MLA_V)
    wk_p = jnp.concatenate(
        [jnp.zeros((depth, MLA_KV_LORA, MLA_HEADS, MLA_ROPE), F32), wkv[..., :MLA_NOPE],
         jnp.zeros((depth, MLA_KV_LORA, MLA_HEADS, LANES - MLA_NOPE - MLA_ROPE), F32)], axis=-1)
    wkv_p = jnp.concatenate(
        [wk_p.reshape(depth, MLA_KV_LORA, MLA_HEADS * LANES),
         wkv[..., MLA_NOPE:].reshape(depth, MLA_KV_LORA, MLA_HEADS * MLA_V)], axis=2).astype(BF16)

    sguw_p = sgu_w.transpose(0, 2, 1, 3).reshape(depth, CHUNK, SGU_HEADS * CHUNK).astype(BF16)
    sgub_p = jnp.repeat(sgu_b.transpose(0, 2, 1), HEAD_DIM, axis=2)

    wo_swa = w_out[:, 3 * GROUP_W:].reshape(depth, SWA_KV_HEADS, SWA_GROUP, HEAD_DIM, D_MODEL)
    wo_swa = wo_swa.transpose(0, 2, 1, 3, 4).reshape(depth, GROUP_W, D_MODEL)
    wo_p = jnp.concatenate([w_out[:, :3 * GROUP_W], wo_swa], axis=1).astype(BF16)

    return {
        "norm1": norm1[:, None, :], "norm2": norm2[:, None, :], "w_in": w_in_main, "w_in_tail": w_in_tail,
        "conv_w": conv_w,
        "sgu_norm": sgu_norm[:, None, :], "sgu_w": sguw_p, "sgu_b": sgub_p,
        "q_norm": mla_q_norm[:, None, :], "w_q_up": wq_p, "kv_norm": mla_kv_norm[:, None, :],
        "w_kv_up": wkv_p, "kpe_place": jnp.eye(LANES, MLA_ROPE, dtype=F32), "w_out": wo_p,
        "w1": mlp_w1.astype(BF16), "w2": mlp_w2.astype(BF16),
    }


def _feature_major(cache):
    b, depth, past, heads, dim = cache.shape
    return cache.transpose(0, 1, 3, 4, 2).reshape(b, depth, heads * dim, past)


def _pick_tile(n, pref):
    while n % pref:
        pref //= 2
    return pref


def kernel(x_prompt, x_sample, cache_mla_ckv, cache_mla_kpe, cache_swa_k, cache_swa_v, c, c_ctx, w_ada, b_ada,
           norm1, norm2, w_in, conv_w, sgu_norm, sgu_w, sgu_b, mla_q_norm, mla_w_q_up, mla_kv_norm, mla_w_kv_up,
           swa_sink, w_out, mlp_w1, mlp_w2, final_norm):
    batch, seq, d = x_prompt.shape
    dec_batch, dec_seq, _ = x_sample.shape
    depth = w_ada.shape[0]
    past = cache_mla_ckv.shape[2]
    assert d == D_MODEL and seq % CHUNK == 0 and dec_seq % (2 * CHUNK) == 0 and 1 + dec_batch <= MOD_ROWS

    c_all = jnp.concatenate(
        [c_ctx[None, :], c, jnp.zeros((MOD_ROWS - 1 - dec_batch, d), F32)], axis=0)
    mod = _ada_call(c_all, w_ada, b_ada).reshape(depth, MOD_ROWS, 1, N_MOD * d)

    pw = _prep_weights(norm1, norm2, w_in, conv_w, sgu_norm, sgu_w, sgu_b, mla_q_norm, mla_w_q_up, mla_kv_norm,
                       mla_w_kv_up, w_out, mlp_w1, mlp_w2)
    tables = _lane_tables(dec_seq)
    cache = (cache_mla_ckv, jnp.swapaxes(cache_mla_kpe, 2, 3),
             _feature_major(cache_swa_k), _feature_major(cache_swa_v))
    fn = final_norm[None, :]

    xp = x_prompt.reshape(batch * seq, d)
    xs = x_sample.reshape(dec_batch * dec_seq, d)
    tile_p = _pick_tile(batch * seq, ROW_SUB)
    tile_s = _pick_tile(dec_seq, ROW_SUB)
    tq_s = _pick_tile(dec_seq, QUERY_SUB)
    ftile_p = _pick_tile(batch * seq, 2 * tile_p)
    ftile_s = _pick_tile(dec_seq, 2 * tile_s)
    atile_s = _pick_tile(dec_seq, 2 * tq_s)
    atile_p = _pick_tile(batch * seq, CTX_SEQS_PER_STEP * seq)
    bsub = _pick_tile(min(ftile_p, ftile_s), BACK_SUB)
    new_cache = [jax.ShapeDtypeStruct((batch, depth) + slab, F32)
                 for slab in ((seq, MLA_KV_LORA), (MLA_ROPE, seq), (LANES, seq), (LANES, seq))]
    for l in range(depth):
        final = l == depth - 1

        fo = _front_call(xp, mod, pw, None, new_cache, layer=l, seq=seq, tile=ftile_p, sub=tile_p, mod_row0=0)
        ab, pc, ysgu = fo[:3]
        new_cache = list(fo[9:13])
        ym, ys = _attn_call(fo[3:9], swa_sink, pw, None, batch=batch, seq=seq, tq=atile_p, sub=seq, layer=l)
        xp = _back_call(xp, mod, ab, pc, ysgu, ym, ys, pw, fn, layer=l, seq=seq, tile=ftile_p, sub=bsub,
                        mod_row0=0, per_seq_mod=False, final=final)

        fo = _front_call(xs, mod, pw, tables, None, layer=l, seq=dec_seq, tile=ftile_s, sub=tile_s, mod_row0=1)
        ab, pc, ysgu = fo[:3]
        ym, ys = _attn_call(fo[3:9], swa_sink, pw, cache, batch=dec_batch, seq=dec_seq, tq=atile_s, sub=tq_s,
                            layer=l)
        xs = _back_call(xs, mod, ab, pc, ysgu, ym, ys, pw, fn, layer=l, seq=dec_seq, tile=ftile_s, sub=bsub,
                        mod_row0=1, per_seq_mod=True, final=final)

    def token_major(t):
        return t.reshape(batch, depth, SWA_KV_HEADS, HEAD_DIM, seq).transpose(0, 1, 4, 2, 3)

    return (xp.reshape(batch, seq, d), xs.reshape(dec_batch, dec_seq, d), new_cache[0],
            jnp.swapaxes(new_cache[1], 2, 3), token_major(new_cache[2]), token_major(new_cache[3]))
```

```python
import functools

import jax
import jax.numpy as jnp
from jax import lax
from jax.experimental import pallas as pl
from jax.experimental.pallas import tpu as pltpu

D_MODEL = 1024
GROUP_W = 256
HEAD_DIM = 64
CONV_WIDTH = 3
CHUNK = 128
SGU_HEADS = 4
MLA_HEADS = 4
MLA_NOPE = 64
MLA_ROPE = 32
MLA_V = 64
MLA_Q_LORA = 256
MLA_KV_LORA = 128
SWA_KV_HEADS = 2
SWA_GROUP = 2
SWA_WINDOW = 128
MLP_HIDDEN = 4096
GRID_W = 64
ROPE_THETA = 10000.0
EPS = 1e-6
N_MOD = 6
MLA_SCALE = (MLA_NOPE + MLA_ROPE) ** -0.5
SWA_SCALE = HEAD_DIM ** -0.5
NEG_INF = -1e30
LOG2E = 1.4426950408889634

LANES = 128
SUBLANES = 8
MOD_ROWS = 16
VMEM_LIMIT = 62 * 1024 * 1024
ROW_SUB = 512
QUERY_SUB = 256
LAT_QUERY_SUBS = 4
BACK_SUB = 256
CTX_SEQS_PER_STEP = 4
ADA_COLS = 1536
HIDDEN_CHUNK = 1024

C_CONV = 0
C_SGU = 768
C_CQ = 1280
IN_MAIN = 1536
T_KV = 0
T_SWA = 256
IN_TAIL = 768
KV_UP_COLS = MLA_HEADS * LANES + MLA_HEADS * MLA_V

BF16 = jnp.bfloat16
F32 = jnp.float32


def _dot(a, b):
    return jnp.dot(a, b, preferred_element_type=F32)


def _dot_nt(a, b):
    return lax.dot_general(a, b, (((1,), (1,)), ((), ())), preferred_element_type=F32)


def _rms(x, g):
    return x * lax.rsqrt(jnp.mean(x * x, axis=-1, keepdims=True) + EPS) * g


def _rope(x, cos, sin_signed, quarter):
    lane = lax.broadcasted_iota(jnp.int32, x.shape, 1)
    first = (lane & quarter) == 0
    rot = jnp.where(first, pltpu.roll(x, LANES - quarter, 1), pltpu.roll(x, quarter, 1))
    return x * cos + rot * sin_signed


def _layer_spec(block, layer):
    nd = len(block)
    return pl.BlockSpec((None,) + tuple(block), lambda *_: (layer,) + (0,) * nd, pipeline_mode=pl.Buffered(1))


def _const_spec(shape):
    nd = len(shape)
    return pl.BlockSpec(shape, lambda *_: (0,) * nd, pipeline_mode=pl.Buffered(1))


def _mod_spec(width, col_block, layer, row_of_step):
    return pl.BlockSpec((None, None, 1, width), lambda *idx: (layer, row_of_step(*idx), 0, col_block))


def _ada_kernel(c_ref, w_ref, b_ref, o_ref):
    c = c_ref[...]
    s = (c * jax.nn.sigmoid(c)).astype(BF16)
    o_ref[...] = _dot(s, w_ref[...].astype(BF16)) + b_ref[...]


def _ada_call(c_all, w_ada, b_ada):
    depth = w_ada.shape[0]
    tn = ADA_COLS
    n_out = N_MOD * D_MODEL
    return pl.pallas_call(
        _ada_kernel,
        out_shape=jax.ShapeDtypeStruct((depth, MOD_ROWS, n_out), F32),
        grid=(depth, n_out // tn),
        in_specs=[
            pl.BlockSpec((MOD_ROWS, D_MODEL), lambda l, j: (0, 0)),
            pl.BlockSpec((None, D_MODEL, tn), lambda l, j: (l, 0, j)),
            pl.BlockSpec((None, 1, tn), lambda l, j: (l, 0, j)),
        ],
        out_specs=pl.BlockSpec((None, MOD_ROWS, tn), lambda l, j: (l, 0, j)),
        compiler_params=pltpu.CompilerParams(
            dimension_semantics=("arbitrary", "arbitrary"), vmem_limit_bytes=VMEM_LIMIT),
        name="ada_mod",
    )(c_all, w_ada, b_ada.reshape(depth, 1, n_out))


def _front_kernel(*refs, rope, cache_mode, sub):
    (x_ref, mod_ref, n1_ref, win_ref, wtail_ref, sgun_ref, sguw_ref, sgub_ref, qn_ref, wq_ref, kvn_ref,
     wkv_ref) = refs[:12]
    pos = 12
    if rope:
        cos_s_ref, sin_s_ref, cos_m_ref, sin_m_ref = refs[pos:pos + 4]
        pos += 4
    cache_out = cache_mode is not None
    if cache_mode == "update":
        pos += 4
    (ab_ref, pc_ref, ysgu_ref, qm_ref, kcat_ref, vm_ref, qs_ref, ks_ref, vs_ref) = refs[pos:pos + 9]
    pos += 9
    if cache_out:
        ckv_o, kpe_o, ksw_o, vsw_o = refs[pos:pos + 4]
        if cache_mode == "init":
            for o_ref in (ckv_o, kpe_o, ksw_o, vsw_o):
                o_ref[:, 1:] = jnp.zeros((o_ref.shape[0], o_ref.shape[1] - 1) + o_ref.shape[2:], F32)

        def put(o_ref, val, st):
            per = val.shape[0] // o_ref.shape[-2]
            slab = val.reshape((per,) + o_ref.shape[-2:])
            if cache_mode == "init":
                o_ref[st * per:(st + 1) * per, 0] = slab
            else:
                o_ref[st * per:(st + 1) * per] = slab

        def put_t(o_ref, val, st):
            width, seq_len = o_ref.shape[-2:]
            per = val.shape[0] // seq_len
            val_t = val.T
            for s in range(per):
                slab = val_t[:width, s * seq_len:(s + 1) * seq_len]
                if cache_mode == "init":
                    o_ref[st * per + s, 0] = slab
                else:
                    o_ref[st * per + s] = slab

    shift = mod_ref[:, 0:D_MODEL]
    scale = mod_ref[:, D_MODEL:2 * D_MODEL]
    head_of_lane = lax.broadcasted_iota(jnp.int32, (CHUNK, GROUP_W), 1) // HEAD_DIM

    for st in range(x_ref.shape[0] // sub):
        rows = slice(st * sub, (st + 1) * sub)
        h = (_rms(x_ref[rows, :], n1_ref[...]) * (1.0 + scale) + shift).astype(BF16)
        if rope:
            cos_s, sin_s = cos_s_ref[rows, :], sin_s_ref[rows, :]
            cos_m, sin_m = cos_m_ref[rows, :], sin_m_ref[rows, :]

        zq = _dot(h, win_ref[:, C_CQ:C_CQ + MLA_Q_LORA])
        zkv = _dot(h, wtail_ref[:, T_KV:T_KV + 2 * LANES])
        zs = _dot(h, win_ref[:, C_SGU:C_SGU + 2 * GROUP_W])

        cqn = _rms(zq, qn_ref[...]).astype(BF16)
        q = _dot(cqn, wq_ref[...]) * (MLA_SCALE * LOG2E)
        for hh in range(MLA_HEADS):
            qh = q[:, hh * LANES:(hh + 1) * LANES]
            if rope:
                qh = _rope(qh, cos_m, sin_m, MLA_ROPE // 4)
            qm_ref[rows, hh * LANES:(hh + 1) * LANES] = qh.astype(BF16)

        ckv = _rms(zkv[:, :MLA_KV_LORA], kvn_ref[...])
        kpe = zkv[:, MLA_KV_LORA:]
        if cache_out:
            put(ckv_o, ckv, st)
            put_t(kpe_o, kpe, st)
        if rope:
            kpe = _rope(kpe, cos_m, sin_m, MLA_ROPE // 4)
        kv = _dot(ckv.astype(BF16), wkv_ref[...])
        for hh in range(MLA_HEADS):
            kcat_ref[rows, hh * LANES:(hh + 1) * LANES] = (kv[:, hh * LANES:(hh + 1) * LANES] + kpe).astype(BF16)
        vm_ref[rows, :] = kv[:, MLA_HEADS * LANES:].astype(BF16)

        zc = _dot(h, win_ref[:, C_CONV:C_CONV + 3 * GROUP_W])
        ab_ref[rows, :] = zc[:, :GROUP_W]
        pc_ref[rows, :] = zc[:, GROUP_W:2 * GROUP_W] * zc[:, 2 * GROUP_W:]

        zw = _dot(h, wtail_ref[:, T_SWA:T_SWA + 4 * LANES])
        sk = zw[:, 2 * LANES:3 * LANES]
        sv_ = zw[:, 3 * LANES:]
        if cache_out:
            put_t(ksw_o, sk, st)
            put_t(vsw_o, sv_, st)
        for g in range(SWA_GROUP):
            qg = zw[:, g * LANES:(g + 1) * LANES] * (SWA_SCALE * LOG2E)
            if rope:
                qg = _rope(qg, cos_s, sin_s, HEAD_DIM // 4)
            qs_ref[rows, g * LANES:(g + 1) * LANES] = qg.astype(BF16)
        if rope:
            sk = _rope(sk, cos_s, sin_s, HEAD_DIM // 4)
        ks_ref[rows, :] = sk.astype(BF16)
        vs_ref[rows, :] = sv_.astype(BF16)

        gz = jax.nn.gelu(zs, approximate=True)
        u = gz[:, :GROUP_W]
        vn = _rms(gz[:, GROUP_W:], sgun_ref[...])
        for c in range(sub // CHUNK):
            crow = slice(c * CHUNK, (c + 1) * CHUNK)
            vc = vn[crow]
            stacked = jnp.concatenate(
                [jnp.where(head_of_lane == hh, vc, 0.0).astype(BF16) for hh in range(SGU_HEADS)], axis=0)
            sv = _dot(sguw_ref[...], stacked) + sgub_ref[...]
            ysgu_ref[st * sub + c * CHUNK:st * sub + (c + 1) * CHUNK, :] = (u[crow] * sv).astype(BF16)


def _front_call(x, mod, pw, tables, cache_stacks, *, layer, seq, tile, sub, mod_row0):
    n = x.shape[0]
    rope = tables is not None
    tiles_per_seq = max(seq // tile, 1)

    def row_spec(width):
        return pl.BlockSpec((tile, width), lambda i: (i, 0))

    def mod_row(i):
        return mod_row0 + (i * tile) // seq if rope else mod_row0

    in_specs = [
        row_spec(D_MODEL),
        _mod_spec(2 * D_MODEL, 0, layer, mod_row),
        _layer_spec((1, D_MODEL), layer),
        _layer_spec((D_MODEL, IN_MAIN), layer),
        _layer_spec((D_MODEL, IN_TAIL), layer),
        _layer_spec((1, GROUP_W), layer),
        _layer_spec((CHUNK, SGU_HEADS * CHUNK), layer),
        _layer_spec((CHUNK, GROUP_W), layer),
        _layer_spec((1, MLA_Q_LORA), layer),
        _layer_spec((MLA_Q_LORA, MLA_HEADS * LANES), layer),
        _layer_spec((1, MLA_KV_LORA), layer),
        _layer_spec((MLA_KV_LORA, KV_UP_COLS), layer),
    ]
    args = [x, mod, pw["norm1"], pw["w_in"], pw["w_in_tail"], pw["sgu_norm"], pw["sgu_w"], pw["sgu_b"],
            pw["q_norm"], pw["w_q_up"], pw["kv_norm"], pw["w_kv_up"]]
    if rope:
        in_specs += [pl.BlockSpec((tile, LANES), lambda i: (i % tiles_per_seq, 0))] * 4
        args += list(tables)

    out_shape = [
        jax.ShapeDtypeStruct((n, GROUP_W), F32),
        jax.ShapeDtypeStruct((n, GROUP_W), F32),
        jax.ShapeDtypeStruct((n, GROUP_W), BF16),
        jax.ShapeDtypeStruct((n, MLA_HEADS * LANES), BF16),
        jax.ShapeDtypeStruct((n, MLA_HEADS * LANES), BF16),
        jax.ShapeDtypeStruct((n, MLA_HEADS * MLA_V), BF16),
        jax.ShapeDtypeStruct((n, 2 * LANES), BF16),
        jax.ShapeDtypeStruct((n, LANES), BF16),
        jax.ShapeDtypeStruct((n, LANES), BF16),
    ]
    out_specs = [row_spec(s.shape[1]) for s in out_shape]
    aliases = {}
    cache_mode = None
    if cache_stacks is not None:
        assert tile % seq == 0
        cache_mode = "init" if layer == 0 else "update"
        for stack in cache_stacks:
            depth, slab = stack.shape[1], tuple(stack.shape[2:])
            out_shape.append(jax.ShapeDtypeStruct(stack.shape, stack.dtype))
            if cache_mode == "init":
                out_specs.append(pl.BlockSpec((tile // seq, depth) + slab, lambda i: (i, 0, 0, 0)))
            else:
                aliases[len(args)] = len(out_shape) - 1
                in_specs.append(pl.BlockSpec(memory_space=pl.ANY))
                args.append(stack)
                out_specs.append(pl.BlockSpec((tile // seq, None) + slab, lambda i: (i, layer, 0, 0)))
    return pl.pallas_call(
        functools.partial(_front_kernel, rope=rope, cache_mode=cache_mode, sub=sub),
        out_shape=out_shape,
        grid=(n // tile,),
        in_specs=in_specs,
        out_specs=out_specs,
        input_output_aliases=aliases,
        compiler_params=pltpu.CompilerParams(
            dimension_semantics=("arbitrary",), vmem_limit_bytes=VMEM_LIMIT),
        name="front_lat" if rope else "front_ctx",
    )(*args)


def _scores(q, pieces):
    scores = []
    for k, _, mask, transposed in pieces:
        s = _dot(q, k) if transposed else _dot_nt(q, k)
        if mask is not None:
            s = jnp.where(mask, s, NEG_INF)
        scores.append(s)
    return scores


def _softmax(scores, sink):
    m = functools.reduce(jnp.maximum, [jnp.max(s, axis=-1, keepdims=True) for s in scores])
    if sink is not None:
        m = jnp.maximum(m, sink)
    denom = jnp.exp2(sink - m) if sink is not None else jnp.zeros_like(m)
    probs = []
    for s in scores:
        p = jnp.exp2(s - m)
        denom = denom + jnp.sum(p, axis=-1, keepdims=True)
        probs.append(p.astype(BF16))
    return probs, 1.0 / denom


def _pv(probs, inv_denom, pieces):
    out = None
    for p, (_, v, _, transposed) in zip(probs, pieces):
        pv = _dot_nt(p, v) if transposed else _dot(p, v)
        out = pv if out is None else out + pv
    return out * inv_denom


def _attn_kernel(*refs, latent, tq, sub, seq, layer):
    sink_ref, qm_ref, kcat_ref, vm_ref, qs_ref, ks_ref, vs_ref = refs[:7]
    pos = 7
    if latent:
        cckv_ref, ckpe_ref, cks_ref, cvs_ref, wkv_ref, place_ref = refs[pos:pos + 6]
        pos += 6
    ym_ref, ys_ref = refs[pos:pos + 2]
    pos += 2
    if latent:
        kc_scr, vc_scr, ksc_scr, vsc_scr = refs[pos:pos + 4]
        qi = pl.program_id(1)

        @pl.when(qi == 0)
        def _expand_cache():
            kv = _dot(cckv_ref[...].astype(BF16), wkv_ref[...])
            kp = _dot(place_ref[...], ckpe_ref[...]).T
            for hh in range(MLA_HEADS):
                kc_scr[:, hh * LANES:(hh + 1) * LANES] = (kv[:, hh * LANES:(hh + 1) * LANES] + kp).astype(BF16)
            vc_scr[...] = kv[:, MLA_HEADS * LANES:].astype(BF16)
            ksc_scr[...] = cks_ref[...].astype(BF16)
            vsc_scr[...] = cvs_ref[...].astype(BF16)

    tasks = []
    kv_of_lane = lax.broadcasted_iota(jnp.int32, (sub, LANES), 1) // HEAD_DIM
    for sb in range(tq // sub):
        qrows = slice(sb * sub, (sb + 1) * sub)
        krows = slice(None) if latent else qrows
        for hh in range(MLA_HEADS):
            cols = slice(hh * LANES, (hh + 1) * LANES)
            pieces = [(kcat_ref[krows, cols], vm_ref[krows, :], None, False)]
            if latent:
                pieces.append((kc_scr[:, cols], vc_scr[...], None, False))
            tasks.append((qm_ref[qrows, cols], pieces, None))
        if latent:
            win = sub + 2 * SWA_WINDOW
            q0 = qi * tq + sb * sub
            start = pl.multiple_of(jnp.clip(q0 - SWA_WINDOW, 0, seq - win), SWA_WINDOW)
            q_pos = q0 + lax.broadcasted_iota(jnp.int32, (sub, win), 0)
            k_pos = start + lax.broadcasted_iota(jnp.int32, (sub, win), 1)
            band_ok = jnp.abs(q_pos - k_pos) <= SWA_WINDOW
            swa_pieces = [(ks_ref[pl.ds(start, win), :], vs_ref[pl.ds(start, win), :], band_ok, False),
                          (ksc_scr[...], vsc_scr[...], None, True)]
        else:
            swa_pieces = [(ks_ref[krows, :], vs_ref[krows, :], None, False)]
        for g in range(SWA_GROUP):
            qg = qs_ref[qrows, g * LANES:(g + 1) * LANES].astype(F32)
            for n in range(SWA_KV_HEADS):
                qh = jnp.where(kv_of_lane == n, qg, 0.0).astype(BF16)
                tasks.append((qh, swa_pieces, sink_ref[layer, n * SWA_GROUP + g] * LOG2E))

    per_block = MLA_HEADS + SWA_GROUP * SWA_KV_HEADS
    head_of_lane = lax.broadcasted_iota(jnp.int32, (sub, MLA_HEADS * MLA_V), 1) // MLA_V
    om = jnp.zeros((sub, MLA_HEADS * MLA_V), F32)
    blk = jnp.zeros((sub, LANES), F32)
    scores = _scores(tasks[0][0], tasks[0][1])
    for i, (_, pieces, sink) in enumerate(tasks):
        probs, inv_denom = _softmax(scores, sink)
        scores = _scores(tasks[i + 1][0], tasks[i + 1][1]) if i + 1 < len(tasks) else None
        o = _pv(probs, inv_denom, pieces)
        sb, j = divmod(i, per_block)
        qrows = slice(sb * sub, (sb + 1) * sub)
        if j < MLA_HEADS:
            om = jnp.where(head_of_lane == j, o, om)
            if j == MLA_HEADS - 1:
                ym_ref[qrows, :] = om.astype(BF16)
        else:
            g, n = divmod(j - MLA_HEADS, SWA_KV_HEADS)
            blk = jnp.where(kv_of_lane == n, o, blk)
            if n == SWA_KV_HEADS - 1:
                ys_ref[qrows, g * LANES:(g + 1) * LANES] = blk.astype(BF16)


def _attn_call(front, sink, pw, cache, *, batch, seq, tq, sub, layer):
    qm, kcat, vm, qs, ks, vs = front
    latent = cache is not None
    assert tq % sub == 0 and (seq % tq == 0 if latent else (sub == seq and (batch * seq) % tq == 0))
    n_q = seq // tq if latent else 1
    kv_rows = seq if latent else tq
    grid = (batch, n_q) if latent else (batch * seq // tq, 1)

    def q_spec(width):
        return pl.BlockSpec((tq, width), lambda b, i: (b * n_q + i, 0))

    def seq_spec(width):
        return pl.BlockSpec((kv_rows, width), lambda b, i: (b, 0))

    in_specs = [
        pl.BlockSpec(memory_space=pltpu.SMEM),
        q_spec(MLA_HEADS * LANES), seq_spec(MLA_HEADS * LANES), seq_spec(MLA_HEADS * MLA_V),
        q_spec(2 * LANES), seq_spec(LANES), seq_spec(LANES),
    ]
    args = [sink, qm, kcat, vm, qs, ks, vs]
    scratch = []
    if latent:
        c_ckv, c_kpe, c_k, c_v = cache
        past = c_ckv.shape[2]

        def cache_spec(width):
            return pl.BlockSpec((None, None, past, width), lambda b, i: (b, layer, 0, 0))

        def cache_spec_t(width):
            return pl.BlockSpec((None, None, width, past), lambda b, i: (b, layer, 0, 0))

        in_specs += [cache_spec(MLA_KV_LORA), cache_spec_t(MLA_ROPE), cache_spec_t(LANES), cache_spec_t(LANES),
                     _layer_spec((MLA_KV_LORA, KV_UP_COLS), layer),
                     _const_spec((LANES, MLA_ROPE))]
        args += [c_ckv, c_kpe, c_k, c_v, pw["w_kv_up"], pw["kpe_place"]]
        scratch = [pltpu.VMEM((past, MLA_HEADS * LANES), BF16), pltpu.VMEM((past, MLA_HEADS * MLA_V), BF16),
                   pltpu.VMEM((LANES, past), BF16), pltpu.VMEM((LANES, past), BF16)]
    n = batch * seq
    return pl.pallas_call(
        functools.partial(_attn_kernel, latent=latent, tq=tq, sub=sub, seq=seq, layer=layer),
        out_shape=[jax.ShapeDtypeStruct((n, MLA_HEADS * MLA_V), BF16),
                   jax.ShapeDtypeStruct((n, 2 * LANES), BF16)],
        grid=grid,
        in_specs=in_specs,
        out_specs=[q_spec(MLA_HEADS * MLA_V), q_spec(2 * LANES)],
        scratch_shapes=scratch,
        compiler_params=pltpu.CompilerParams(
            dimension_semantics=("arbitrary", "arbitrary"), vmem_limit_bytes=VMEM_LIMIT),
        name="attn_lat" if latent else "attn_ctx",
    )(*args)


def _back_kernel(x_ref, mod_ref, ab_ref, pc_ref, pprev_ref, pnext_ref, ysgu_ref, ym_ref, ys_ref, convw_ref,
                 wout_ref, n2_ref, w1_ref, w2_ref, fn_ref, o_ref, *, seq, sub, final, hidden_chunk):
    t = x_ref.shape[0]
    row = pl.program_id(0) * t + lax.broadcasted_iota(jnp.int32, (t, GROUP_W), 0)
    local = lax.broadcasted_iota(jnp.int32, (t, GROUP_W), 0)
    pc = pc_ref[...]
    prev_row = pprev_ref[SUBLANES - 1:SUBLANES, :]
    next_row = pnext_ref[0:1, :]
    up = jnp.where(local == 0, prev_row, pltpu.roll(pc, 1, 0))
    up = jnp.where(row % seq == 0, 0.0, up)
    dn = jnp.where(local == t - 1, next_row, pltpu.roll(pc, t - 1, 0))
    dn = jnp.where((row + 1) % seq == 0, 0.0, dn)
    conv = convw_ref[0:1, :] * up + convw_ref[1:2, :] * pc + convw_ref[2:3, :] * dn
    y_conv = (ab_ref[...] * conv).astype(BF16)

    y = jnp.concatenate([y_conv, ysgu_ref[...], ym_ref[...], ys_ref[...]], axis=1)
    gate1 = mod_ref[:, 2 * D_MODEL:3 * D_MODEL]
    shift2 = mod_ref[:, 3 * D_MODEL:4 * D_MODEL]
    scale2 = mod_ref[:, 4 * D_MODEL:5 * D_MODEL]
    gate2 = mod_ref[:, 5 * D_MODEL:6 * D_MODEL]
    subs = [slice(k * sub, (k + 1) * sub) for k in range(t // sub)]
    x1s = [x_ref[r, :] + gate1 * _dot(y[r], wout_ref[...]) for r in subs]
    for r, x1 in zip(subs, x1s):
        h2 = (_rms(x1, n2_ref[...]) * (1.0 + scale2) + shift2).astype(BF16)
        acc = jnp.zeros((sub, D_MODEL), F32)
        for c in range(MLP_HIDDEN // hidden_chunk):
            cols = slice(c * hidden_chunk, (c + 1) * hidden_chunk)
            hid = jnp.maximum(_dot(h2, w1_ref[:, cols]), 0.0)
            acc = acc + _dot((hid * hid).astype(BF16), w2_ref[cols, :])
        x2 = x1 + gate2 * acc
        if final:
            x2 = _rms(x2, fn_ref[...])
        o_ref[r, :] = x2


def _back_call(x, mod, ab, pc, ysgu, ym, ys, pw, final_norm, *, layer, seq, tile, sub, mod_row0, per_seq_mod,
               final):
    n = x.shape[0]
    halo_blocks = n // SUBLANES
    per_tile = tile // SUBLANES

    def row_spec(width):
        return pl.BlockSpec((tile, width), lambda i: (i, 0))

    def mod_row(i):
        return mod_row0 + (i * tile) // seq if per_seq_mod else mod_row0

    in_specs = [
        row_spec(D_MODEL),
        _mod_spec(N_MOD * D_MODEL, 0, layer, mod_row),
        row_spec(GROUP_W), row_spec(GROUP_W),
        pl.BlockSpec((SUBLANES, GROUP_W), lambda i: (jnp.maximum(i * per_tile - 1, 0), 0)),
        pl.BlockSpec((SUBLANES, GROUP_W), lambda i: (jnp.minimum((i + 1) * per_tile, halo_blocks - 1), 0)),
        row_spec(GROUP_W), row_spec(GROUP_W), row_spec(GROUP_W),
        _layer_spec((CONV_WIDTH, GROUP_W), layer),
        _layer_spec((D_MODEL, D_MODEL), layer),
        _layer_spec((1, D_MODEL), layer),
        _layer_spec((D_MODEL, MLP_HIDDEN), layer),
        _layer_spec((MLP_HIDDEN, D_MODEL), layer),
        _const_spec((1, D_MODEL)),
    ]
    return pl.pallas_call(
        functools.partial(_back_kernel, seq=seq, sub=sub, final=final, hidden_chunk=HIDDEN_CHUNK),
        out_shape=jax.ShapeDtypeStruct((n, D_MODEL), F32),
        grid=(n // tile,),
        in_specs=in_specs,
        out_specs=row_spec(D_MODEL),
        compiler_params=pltpu.CompilerParams(
            dimension_semantics=("arbitrary",), vmem_limit_bytes=VMEM_LIMIT),
        name="back",
    )(x, mod, ab, pc, pc, pc, ysgu, ym, ys, pw["conv_w"], pw["w_out"], pw["norm2"], pw["w1"], pw["w2"],
      final_norm)


def _rope_tables(n_tokens, rot_dim):
    rows = n_tokens // GRID_W
    row = jnp.repeat(jnp.arange(rows, dtype=F32), GRID_W)
    col = jnp.tile(jnp.arange(GRID_W, dtype=F32), rows)
    half = rot_dim // 2
    inv_freq = ROPE_THETA ** (-jnp.arange(0, half, 2, dtype=F32) / half)
    ang_r = row[:, None] * inv_freq[None, :]
    ang_c = col[:, None] * inv_freq[None, :]
    ang = jnp.concatenate([ang_r, ang_r, ang_c, ang_c], axis=-1)
    sign = jnp.where((jnp.arange(rot_dim) & (rot_dim // 4)) == 0, -1.0, 1.0).astype(F32)
    return jnp.cos(ang), jnp.sin(ang) * sign[None, :]


def _lane_tables(n_tokens):
    cos_s, sin_s = _rope_tables(n_tokens, HEAD_DIM)
    cos_s = jnp.tile(cos_s, (1, LANES // HEAD_DIM))
    sin_s = jnp.tile(sin_s, (1, LANES // HEAD_DIM))
    cos_m, sin_m = _rope_tables(n_tokens, MLA_ROPE)
    pad = LANES - MLA_ROPE
    cos_m = jnp.concatenate([cos_m, jnp.ones((n_tokens, pad), F32)], axis=1)
    sin_m = jnp.concatenate([sin_m, jnp.zeros((n_tokens, pad), F32)], axis=1)
    return cos_s, sin_s, cos_m, sin_m


def _prep_weights(norm1, norm2, w_in, conv_w, sgu_norm, sgu_w, sgu_b, mla_q_norm, mla_w_q_up, mla_kv_norm,
                  mla_w_kv_up, w_out, mlp_w1, mlp_w2):
    depth, d, _ = w_in.shape
    s_kpe = IN_MAIN + MLA_KV_LORA
    s_q = s_kpe + MLA_ROPE
    w_in_main = w_in[:, :, :IN_MAIN].astype(BF16)
    sq = w_in[:, :, s_q:s_q + 4 * HEAD_DIM].reshape(depth, d, SWA_KV_HEADS, SWA_GROUP, HEAD_DIM)
    sq = sq.transpose(0, 1, 3, 2, 4).reshape(depth, d, 4 * HEAD_DIM)
    w_in_tail = jnp.concatenate(
        [w_in[:, :, IN_MAIN:s_q], jnp.zeros((depth, d, LANES - MLA_ROPE), F32), sq,
         w_in[:, :, s_q + 4 * HEAD_DIM:]], axis=2).astype(BF16)

    wq = mla_w_q_up.reshape(depth, MLA_Q_LORA, MLA_HEADS, MLA_NOPE + MLA_ROPE)
    wq_p = jnp.concatenate(
        [wq[..., MLA_NOPE:], wq[..., :MLA_NOPE],
         jnp.zeros((depth, MLA_Q_LORA, MLA_HEADS, LANES - MLA_NOPE - MLA_ROPE), F32)], axis=-1)
    wq_p = wq_p.reshape(depth, MLA_Q_LORA, MLA_HEADS * LANES).astype(BF16)

    wkv = mla_w_kv_up.reshape(depth, MLA_KV_LORA, MLA_HEADS, MLA_NOPE + MLA_V)
    wk_p = jnp.concatenate(
        [jnp.zeros((depth, MLA_KV_LORA, MLA_HEADS, MLA_ROPE), F32), wkv[..., :MLA_NOPE],
         jnp.zeros((depth, MLA_KV_LORA, MLA_HEADS, LANES - MLA_NOPE - MLA_ROPE), F32)], axis=-1)
    wkv_p = jnp.concatenate(
        [wk_p.reshape(depth, MLA_KV_LORA, MLA_HEADS * LANES),
         wkv[..., MLA_NOPE:].reshape(depth, MLA_KV_LORA, MLA_HEADS * MLA_V)], axis=2).astype(BF16)

    sguw_p = sgu_w.transpose(0, 2, 1, 3).reshape(depth, CHUNK, SGU_HEADS * CHUNK).astype(BF16)
    sgub_p = jnp.repeat(sgu_b.transpose(0, 2, 1), HEAD_DIM, axis=2)

    wo_swa = w_out[:, 3 * GROUP_W:].reshape(depth, SWA_KV_HEADS, SWA_GROUP, HEAD_DIM, D_MODEL)
    wo_swa = wo_swa.transpose(0, 2, 1, 3, 4).reshape(depth, GROUP_W, D_MODEL)
    wo_p = jnp.concatenate([w_out[:, :3 * GROUP_W], wo_swa], axis=1).astype(BF16)

    return {
        "norm1": norm1[:, None, :], "norm2": norm2[:, None, :], "w_in": w_in_main, "w_in_tail": w_in_tail,
        "conv_w": conv_w,
        "sgu_norm": sgu_norm[:, None, :], "sgu_w": sguw_p, "sgu_b": sgub_p,
        "q_norm": mla_q_norm[:, None, :], "w_q_up": wq_p, "kv_norm": mla_kv_norm[:, None, :],
        "w_kv_up": wkv_p, "kpe_place": jnp.eye(LANES, MLA_ROPE, dtype=F32), "w_out": wo_p,
        "w1": mlp_w1.astype(BF16), "w2": mlp_w2.astype(BF16),
    }


def _feature_major(cache):
    b, depth, past, heads, dim = cache.shape
    return cache.transpose(0, 1, 3, 4, 2).reshape(b, depth, heads * dim, past)


def _pick_tile(n, pref):
    while n % pref:
        pref //= 2
    return pref


def kernel(x_prompt, x_sample, cache_mla_ckv, cache_mla_kpe, cache_swa_k, cache_swa_v, c, c_ctx, w_ada, b_ada,
           norm1, norm2, w_in, conv_w, sgu_norm, sgu_w, sgu_b, mla_q_norm, mla_w_q_up, mla_kv_norm, mla_w_kv_up,
           swa_sink, w_out, mlp_w1, mlp_w2, final_norm):
    batch, seq, d = x_prompt.shape
    dec_batch, dec_seq, _ = x_sample.shape
    depth = w_ada.shape[0]
    past = cache_mla_ckv.shape[2]
    assert d == D_MODEL and seq % CHUNK == 0 and dec_seq % (2 * CHUNK) == 0 and 1 + dec_batch <= MOD_ROWS

    c_all = jnp.concatenate(
        [c_ctx[None, :], c, jnp.zeros((MOD_ROWS - 1 - dec_batch, d), F32)], axis=0)
    mod = _ada_call(c_all, w_ada, b_ada).reshape(depth, MOD_ROWS, 1, N_MOD * d)

    pw = _prep_weights(norm1, norm2, w_in, conv_w, sgu_norm, sgu_w, sgu_b, mla_q_norm, mla_w_q_up, mla_kv_norm,
                       mla_w_kv_up, w_out, mlp_w1, mlp_w2)
    tables = _lane_tables(dec_seq)
    cache = (cache_mla_ckv, jnp.swapaxes(cache_mla_kpe, 2, 3),
             _feature_major(cache_swa_k), _feature_major(cache_swa_v))
    fn = final_norm[None, :]

    xp = x_prompt.reshape(batch * seq, d)
    xs = x_sample.reshape(dec_batch * dec_seq, d)
    tile_p = _pick_tile(batch * seq, ROW_SUB)
    tile_s = _pick_tile(dec_seq, ROW_SUB)
    tq_s = _pick_tile(dec_seq, QUERY_SUB)
    ftile_p = _pick_tile(batch * seq, 2 * tile_p)
    ftile_s = _pick_tile(dec_seq, 2 * tile_s)
    atile_s = _pick_tile(dec_seq, LAT_QUERY_SUBS * tq_s)
    atile_p = _pick_tile(batch * seq, CTX_SEQS_PER_STEP * seq)
    bsub = _pick_tile(min(ftile_p, ftile_s), BACK_SUB)
    new_cache = [jax.ShapeDtypeStruct((batch, depth) + slab, F32)
                 for slab in ((seq, MLA_KV_LORA), (MLA_ROPE, seq), (LANES, seq), (LANES, seq))]
    for l in range(depth):
        final = l == depth - 1

        fo = _front_call(xp, mod, pw, None, new_cache, layer=l, seq=seq, tile=ftile_p, sub=tile_p, mod_row0=0)
        ab, pc, ysgu = fo[:3]
        new_cache = list(fo[9:13])
        ym, ys = _attn_call(fo[3:9], swa_sink, pw, None, batch=batch, seq=seq, tq=atile_p, sub=seq, layer=l)
        xp = _back_call(xp, mod, ab, pc, ysgu, ym, ys, pw, fn, layer=l, seq=seq, tile=ftile_p, sub=bsub,
                        mod_row0=0, per_seq_mod=False, final=final)

        fo = _front_call(xs, mod, pw, tables, None, layer=l, seq=dec_seq, tile=ftile_s, sub=tile_s, mod_row0=1)
        ab, pc, ysgu = fo[:3]
        ym, ys = _attn_call(fo[3:9], swa_sink, pw, cache, batch=dec_batch, seq=dec_seq, tq=atile_s, sub=tq_s,
                            layer=l)
        xs = _back_call(xs, mod, ab, pc, ysgu, ym, ys, pw, fn, layer=l, seq=dec_seq, tile=ftile_s, sub=bsub,
                        mod_row0=1, per_seq_mod=True, final=final)

    def token_major(t):
        return t.reshape(batch, depth, SWA_KV_HEADS, HEAD_DIM, seq).transpose(0, 1, 4, 2, 3)

    return (xp.reshape(batch, seq, d), xs.reshape(dec_batch, dec_seq, d), new_cache[0],
            jnp.swapaxes(new_cache[1], 2, 3), token_major(new_cache[2]), token_major(new_cache[3]))
```

```python
import functools

import jax
import jax.numpy as jnp
from jax import lax
from jax.experimental import pallas as pl
from jax.experimental.pallas import tpu as pltpu

D_MODEL = 1024
GROUP_W = 256
HEAD_DIM = 64
CONV_WIDTH = 3
CHUNK = 128
SGU_HEADS = 4
MLA_HEADS = 4
MLA_NOPE = 64
MLA_ROPE = 32
MLA_V = 64
MLA_Q_LORA = 256
MLA_KV_LORA = 128
SWA_KV_HEADS = 2
SWA_GROUP = 2
SWA_WINDOW = 128
MLP_HIDDEN = 4096
GRID_W = 64
ROPE_THETA = 10000.0
EPS = 1e-6
N_MOD = 6
MLA_SCALE = (MLA_NOPE + MLA_ROPE) ** -0.5
SWA_SCALE = HEAD_DIM ** -0.5
NEG_INF = -1e30
LOG2E = 1.4426950408889634

LANES = 128
SUBLANES = 8
MOD_ROWS = 16
VMEM_LIMIT = 56 * 1024 * 1024
ROW_SUB = 512
QUERY_SUB = 256
BACK_SUB = 256
CTX_SEQS_PER_STEP = 4
ADA_COLS = 1536
HIDDEN_CHUNK = 1024

C_CONV = 0
C_SGU = 768
C_CQ = 1280
IN_MAIN = 1536
T_KV = 0
T_SWA = 256
IN_TAIL = 768
KV_UP_COLS = MLA_HEADS * LANES + MLA_HEADS * MLA_V

BF16 = jnp.bfloat16
F32 = jnp.float32


def _dot(a, b):
    return jnp.dot(a, b, preferred_element_type=F32)


def _dot_nt(a, b):
    return lax.dot_general(a, b, (((1,), (1,)), ((), ())), preferred_element_type=F32)


def _rms(x, g):
    return x * lax.rsqrt(jnp.mean(x * x, axis=-1, keepdims=True) + EPS) * g


def _rope(x, cos, sin_signed, quarter):
    lane = lax.broadcasted_iota(jnp.int32, x.shape, 1)
    first = (lane & quarter) == 0
    rot = jnp.where(first, pltpu.roll(x, LANES - quarter, 1), pltpu.roll(x, quarter, 1))
    return x * cos + rot * sin_signed


def _layer_spec(block, layer):
    nd = len(block)
    return pl.BlockSpec((None,) + tuple(block), lambda *_: (layer,) + (0,) * nd, pipeline_mode=pl.Buffered(1))


def _const_spec(shape):
    nd = len(shape)
    return pl.BlockSpec(shape, lambda *_: (0,) * nd, pipeline_mode=pl.Buffered(1))


def _mod_spec(width, col_block, layer, row_of_step):
    return pl.BlockSpec((None, None, 1, width), lambda *idx: (layer, row_of_step(*idx), 0, col_block))


def _ada_kernel(c_ref, w_ref, b_ref, o_ref):
    c = c_ref[...]
    s = (c * jax.nn.sigmoid(c)).astype(BF16)
    o_ref[...] = _dot(s, w_ref[...].astype(BF16)) + b_ref[...]


def _ada_call(c_all, w_ada, b_ada):
    depth = w_ada.shape[0]
    tn = ADA_COLS
    n_out = N_MOD * D_MODEL
    return pl.pallas_call(
        _ada_kernel,
        out_shape=jax.ShapeDtypeStruct((depth, MOD_ROWS, n_out), F32),
        grid=(depth, n_out // tn),
        in_specs=[
            pl.BlockSpec((MOD_ROWS, D_MODEL), lambda l, j: (0, 0)),
            pl.BlockSpec((None, D_MODEL, tn), lambda l, j: (l, 0, j)),
            pl.BlockSpec((None, 1, tn), lambda l, j: (l, 0, j)),
        ],
        out_specs=pl.BlockSpec((None, MOD_ROWS, tn), lambda l, j: (l, 0, j)),
        compiler_params=pltpu.CompilerParams(
            dimension_semantics=("arbitrary", "arbitrary"), vmem_limit_bytes=VMEM_LIMIT),
        name="ada_mod",
    )(c_all, w_ada, b_ada.reshape(depth, 1, n_out))


def _front_kernel(*refs, rope, cache_mode, sub):
    (x_ref, mod_ref, n1_ref, win_ref, wtail_ref, sgun_ref, sguw_ref, sgub_ref, qn_ref, wq_ref, kvn_ref,
     wkv_ref) = refs[:12]
    pos = 12
    if rope:
        cos_s_ref, sin_s_ref, cos_m_ref, sin_m_ref = refs[pos:pos + 4]
        pos += 4
    cache_out = cache_mode is not None
    if cache_mode == "update":
        pos += 4
    (ab_ref, pc_ref, ysgu_ref, qm_ref, kcat_ref, vm_ref, qs_ref, ks_ref, vs_ref) = refs[pos:pos + 9]
    pos += 9
    if cache_out:
        ckv_o, kpe_o, ksw_o, vsw_o = refs[pos:pos + 4]
        if cache_mode == "init":
            for o_ref in (ckv_o, kpe_o, ksw_o, vsw_o):
                o_ref[:, 1:] = jnp.zeros((o_ref.shape[0], o_ref.shape[1] - 1) + o_ref.shape[2:], F32)

        def put(o_ref, val, st):
            per = val.shape[0] // o_ref.shape[-2]
            slab = val.reshape((per,) + o_ref.shape[-2:])
            if cache_mode == "init":
                o_ref[st * per:(st + 1) * per, 0] = slab
            else:
                o_ref[st * per:(st + 1) * per] = slab

        def put_t(o_ref, val, st):
            width, seq_len = o_ref.shape[-2:]
            per = val.shape[0] // seq_len
            val_t = val.T
            for s in range(per):
                slab = val_t[:width, s * seq_len:(s + 1) * seq_len]
                if cache_mode == "init":
                    o_ref[st * per + s, 0] = slab
                else:
                    o_ref[st * per + s] = slab

    shift = mod_ref[:, 0:D_MODEL]
    scale = mod_ref[:, D_MODEL:2 * D_MODEL]
    head_of_lane = lax.broadcasted_iota(jnp.int32, (CHUNK, GROUP_W), 1) // HEAD_DIM

    for st in range(x_ref.shape[0] // sub):
        rows = slice(st * sub, (st + 1) * sub)
        h = (_rms(x_ref[rows, :], n1_ref[...]) * (1.0 + scale) + shift).astype(BF16)
        if rope:
            cos_s, sin_s = cos_s_ref[rows, :], sin_s_ref[rows, :]
            cos_m, sin_m = cos_m_ref[rows, :], sin_m_ref[rows, :]

        zq = _dot(h, win_ref[:, C_CQ:C_CQ + MLA_Q_LORA])
        zkv = _dot(h, wtail_ref[:, T_KV:T_KV + 2 * LANES])
        zs = _dot(h, win_ref[:, C_SGU:C_SGU + 2 * GROUP_W])

        cqn = _rms(zq, qn_ref[...]).astype(BF16)
        q = _dot(cqn, wq_ref[...]) * (MLA_SCALE * LOG2E)
        for hh in range(MLA_HEADS):
            qh = q[:, hh * LANES:(hh + 1) * LANES]
            if rope:
                qh = _rope(qh, cos_m, sin_m, MLA_ROPE // 4)
            qm_ref[rows, hh * LANES:(hh + 1) * LANES] = qh.astype(BF16)

        ckv = _rms(zkv[:, :MLA_KV_LORA], kvn_ref[...])
        kpe = zkv[:, MLA_KV_LORA:]
        if cache_out:
            put(ckv_o, ckv, st)
            put_t(kpe_o, kpe, st)
        if rope:
            kpe = _rope(kpe, cos_m, sin_m, MLA_ROPE // 4)
        kv = _dot(ckv.astype(BF16), wkv_ref[...])
        for hh in range(MLA_HEADS):
            kcat_ref[rows, hh * LANES:(hh + 1) * LANES] = (kv[:, hh * LANES:(hh + 1) * LANES] + kpe).astype(BF16)
        vm_ref[rows, :] = kv[:, MLA_HEADS * LANES:].astype(BF16)

        zc = _dot(h, win_ref[:, C_CONV:C_CONV + 3 * GROUP_W])
        ab_ref[rows, :] = zc[:, :GROUP_W]
        pc_ref[rows, :] = zc[:, GROUP_W:2 * GROUP_W] * zc[:, 2 * GROUP_W:]

        zw = _dot(h, wtail_ref[:, T_SWA:T_SWA + 4 * LANES])
        sk = zw[:, 2 * LANES:3 * LANES]
        sv_ = zw[:, 3 * LANES:]
        if cache_out:
            put_t(ksw_o, sk, st)
            put_t(vsw_o, sv_, st)
        for g in range(SWA_GROUP):
            qg = zw[:, g * LANES:(g + 1) * LANES] * (SWA_SCALE * LOG2E)
            if rope:
                qg = _rope(qg, cos_s, sin_s, HEAD_DIM // 4)
            qs_ref[rows, g * LANES:(g + 1) * LANES] = qg.astype(BF16)
        if rope:
            sk = _rope(sk, cos_s, sin_s, HEAD_DIM // 4)
        ks_ref[rows, :] = sk.astype(BF16)
        vs_ref[rows, :] = sv_.astype(BF16)

        gz = jax.nn.gelu(zs, approximate=True)
        u = gz[:, :GROUP_W]
        vn = _rms(gz[:, GROUP_W:], sgun_ref[...])
        for c in range(sub // CHUNK):
            crow = slice(c * CHUNK, (c + 1) * CHUNK)
            vc = vn[crow]
            stacked = jnp.concatenate(
                [jnp.where(head_of_lane == hh, vc, 0.0).astype(BF16) for hh in range(SGU_HEADS)], axis=0)
            sv = _dot(sguw_ref[...], stacked) + sgub_ref[...]
            ysgu_ref[st * sub + c * CHUNK:st * sub + (c + 1) * CHUNK, :] = (u[crow] * sv).astype(BF16)


def _front_call(x, mod, pw, tables, cache_stacks, *, layer, seq, tile, sub, mod_row0):
    n = x.shape[0]
    rope = tables is not None
    tiles_per_seq = max(seq // tile, 1)

    def row_spec(width):
        return pl.BlockSpec((tile, width), lambda i: (i, 0))

    def mod_row(i):
        return mod_row0 + (i * tile) // seq if rope else mod_row0

    in_specs = [
        row_spec(D_MODEL),
        _mod_spec(2 * D_MODEL, 0, layer, mod_row),
        _layer_spec((1, D_MODEL), layer),
        _layer_spec((D_MODEL, IN_MAIN), layer),
        _layer_spec((D_MODEL, IN_TAIL), layer),
        _layer_spec((1, GROUP_W), layer),
        _layer_spec((CHUNK, SGU_HEADS * CHUNK), layer),
        _layer_spec((CHUNK, GROUP_W), layer),
        _layer_spec((1, MLA_Q_LORA), layer),
        _layer_spec((MLA_Q_LORA, MLA_HEADS * LANES), layer),
        _layer_spec((1, MLA_KV_LORA), layer),
        _layer_spec((MLA_KV_LORA, KV_UP_COLS), layer),
    ]
    args = [x, mod, pw["norm1"], pw["w_in"], pw["w_in_tail"], pw["sgu_norm"], pw["sgu_w"], pw["sgu_b"],
            pw["q_norm"], pw["w_q_up"], pw["kv_norm"], pw["w_kv_up"]]
    if rope:
        in_specs += [pl.BlockSpec((tile, LANES), lambda i: (i % tiles_per_seq, 0))] * 4
        args += list(tables)

    out_shape = [
        jax.ShapeDtypeStruct((n, GROUP_W), F32),
        jax.ShapeDtypeStruct((n, GROUP_W), F32),
        jax.ShapeDtypeStruct((n, GROUP_W), BF16),
        jax.ShapeDtypeStruct((n, MLA_HEADS * LANES), BF16),
        jax.ShapeDtypeStruct((n, MLA_HEADS * LANES), BF16),
        jax.ShapeDtypeStruct((n, MLA_HEADS * MLA_V), BF16),
        jax.ShapeDtypeStruct((n, 2 * LANES), BF16),
        jax.ShapeDtypeStruct((n, LANES), BF16),
        jax.ShapeDtypeStruct((n, LANES), BF16),
    ]
    out_specs = [row_spec(s.shape[1]) for s in out_shape]
    aliases = {}
    cache_mode = None
    if cache_stacks is not None:
        assert tile % seq == 0
        cache_mode = "init" if layer == 0 else "update"
        for stack in cache_stacks:
            depth, slab = stack.shape[1], tuple(stack.shape[2:])
            out_shape.append(jax.ShapeDtypeStruct(stack.shape, stack.dtype))
            if cache_mode == "init":
                out_specs.append(pl.BlockSpec((tile // seq, depth) + slab, lambda i: (i, 0, 0, 0)))
            else:
                aliases[len(args)] = len(out_shape) - 1
                in_specs.append(pl.BlockSpec(memory_space=pl.ANY))
                args.append(stack)
                out_specs.append(pl.BlockSpec((tile // seq, None) + slab, lambda i: (i, layer, 0, 0)))
    return pl.pallas_call(
        functools.partial(_front_kernel, rope=rope, cache_mode=cache_mode, sub=sub),
        out_shape=out_shape,
        grid=(n // tile,),
        in_specs=in_specs,
        out_specs=out_specs,
        input_output_aliases=aliases,
        compiler_params=pltpu.CompilerParams(
            dimension_semantics=("arbitrary",), vmem_limit_bytes=VMEM_LIMIT),
        name="front_lat" if rope else "front_ctx",
    )(*args)


def _scores(q, pieces):
    scores = []
    for k, _, mask, transposed in pieces:
        s = _dot(q, k) if transposed else _dot_nt(q, k)
        if mask is not None:
            s = jnp.where(mask, s, NEG_INF)
        scores.append(s)
    return scores


def _softmax(scores, sink):
    m = functools.reduce(jnp.maximum, [jnp.max(s, axis=-1, keepdims=True) for s in scores])
    if sink is not None:
        m = jnp.maximum(m, sink)
    denom = jnp.exp2(sink - m) if sink is not None else jnp.zeros_like(m)
    probs = []
    for s in scores:
        p = jnp.exp2(s - m)
        denom = denom + jnp.sum(p, axis=-1, keepdims=True)
        probs.append(p.astype(BF16))
    return probs, 1.0 / denom


def _pv(probs, inv_denom, pieces):
    out = None
    for p, (_, v, _, transposed) in zip(probs, pieces):
        pv = _dot_nt(p, v) if transposed else _dot(p, v)
        out = pv if out is None else out + pv
    return out * inv_denom


def _attn_kernel(*refs, latent, tq, sub, seq, layer):
    sink_ref, qm_ref, kcat_ref, vm_ref, qs_ref, ks_ref, vs_ref = refs[:7]
    pos = 7
    if latent:
        cckv_ref, ckpe_ref, cks_ref, cvs_ref, wkv_ref, place_ref = refs[pos:pos + 6]
        pos += 6
    ym_ref, ys_ref = refs[pos:pos + 2]
    pos += 2
    if latent:
        kc_scr, vc_scr, ksc_scr, vsc_scr = refs[pos:pos + 4]
        qi = pl.program_id(1)

        @pl.when(qi == 0)
        def _expand_cache():
            kv = _dot(cckv_ref[...].astype(BF16), wkv_ref[...])
            kp = _dot(place_ref[...], ckpe_ref[...]).T
            for hh in range(MLA_HEADS):
                kc_scr[:, hh * LANES:(hh + 1) * LANES] = (kv[:, hh * LANES:(hh + 1) * LANES] + kp).astype(BF16)
            vc_scr[...] = kv[:, MLA_HEADS * LANES:].astype(BF16)
            ksc_scr[...] = cks_ref[...].astype(BF16)
            vsc_scr[...] = cvs_ref[...].astype(BF16)

    tasks = []
    kv_of_lane = lax.broadcasted_iota(jnp.int32, (sub, LANES), 1) // HEAD_DIM
    for sb in range(tq // sub):
        qrows = slice(sb * sub, (sb + 1) * sub)
        krows = slice(None) if latent else qrows
        for hh in range(MLA_HEADS):
            cols = slice(hh * LANES, (hh + 1) * LANES)
            pieces = [(kcat_ref[krows, cols], vm_ref[krows, :], None, False)]
            if latent:
                pieces.append((kc_scr[:, cols], vc_scr[...], None, False))
            tasks.append((qm_ref[qrows, cols], pieces, None))
        if latent:
            win = sub + 2 * SWA_WINDOW
            q0 = qi * tq + sb * sub
            start = pl.multiple_of(jnp.clip(q0 - SWA_WINDOW, 0, seq - win), SWA_WINDOW)
            q_pos = q0 + lax.broadcasted_iota(jnp.int32, (sub, win), 0)
            k_pos = start + lax.broadcasted_iota(jnp.int32, (sub, win), 1)
            band_ok = jnp.abs(q_pos - k_pos) <= SWA_WINDOW
            swa_pieces = [(ks_ref[pl.ds(start, win), :], vs_ref[pl.ds(start, win), :], band_ok, False),
                          (ksc_scr[...], vsc_scr[...], None, True)]
        else:
            swa_pieces = [(ks_ref[krows, :], vs_ref[krows, :], None, False)]
        for g in range(SWA_GROUP):
            qg = qs_ref[qrows, g * LANES:(g + 1) * LANES].astype(F32)
            for n in range(SWA_KV_HEADS):
                qh = jnp.where(kv_of_lane == n, qg, 0.0).astype(BF16)
                tasks.append((qh, swa_pieces, sink_ref[layer, n * SWA_GROUP + g] * LOG2E))

    per_block = MLA_HEADS + SWA_GROUP * SWA_KV_HEADS
    head_of_lane = lax.broadcasted_iota(jnp.int32, (sub, MLA_HEADS * MLA_V), 1) // MLA_V
    om = jnp.zeros((sub, MLA_HEADS * MLA_V), F32)
    blk = jnp.zeros((sub, LANES), F32)
    scores = _scores(tasks[0][0], tasks[0][1])
    for i, (_, pieces, sink) in enumerate(tasks):
        probs, inv_denom = _softmax(scores, sink)
        scores = _scores(tasks[i + 1][0], tasks[i + 1][1]) if i + 1 < len(tasks) else None
        o = _pv(probs, inv_denom, pieces)
        sb, j = divmod(i, per_block)
        qrows = slice(sb * sub, (sb + 1) * sub)
        if j < MLA_HEADS:
            om = jnp.where(head_of_lane == j, o, om)
            if j == MLA_HEADS - 1:
                ym_ref[qrows, :] = om.astype(BF16)
        else:
            g, n = divmod(j - MLA_HEADS, SWA_KV_HEADS)
            blk = jnp.where(kv_of_lane == n, o, blk)
            if n == SWA_KV_HEADS - 1:
                ys_ref[qrows, g * LANES:(g + 1) * LANES] = blk.astype(BF16)


def _attn_call(front, sink, pw, cache, *, batch, seq, tq, sub, layer):
    qm, kcat, vm, qs, ks, vs = front
    latent = cache is not None
    assert tq % sub == 0 and (seq % tq == 0 if latent else (sub == seq and (batch * seq) % tq == 0))
    n_q = seq // tq if latent else 1
    kv_rows = seq if latent else tq
    grid = (batch, n_q) if latent else (batch * seq // tq, 1)

    def q_spec(width):
        return pl.BlockSpec((tq, width), lambda b, i: (b * n_q + i, 0))

    def seq_spec(width):
        return pl.BlockSpec((kv_rows, width), lambda b, i: (b, 0))

    in_specs = [
        pl.BlockSpec(memory_space=pltpu.SMEM),
        q_spec(MLA_HEADS * LANES), seq_spec(MLA_HEADS * LANES), seq_spec(MLA_HEADS * MLA_V),
        q_spec(2 * LANES), seq_spec(LANES), seq_spec(LANES),
    ]
    args = [sink, qm, kcat, vm, qs, ks, vs]
    scratch = []
    if latent:
        c_ckv, c_kpe, c_k, c_v = cache
        past = c_ckv.shape[2]

        def cache_spec(width):
            return pl.BlockSpec((None, None, past, width), lambda b, i: (b, layer, 0, 0))

        def cache_spec_t(width):
            return pl.BlockSpec((None, None, width, past), lambda b, i: (b, layer, 0, 0))

        in_specs += [cache_spec(MLA_KV_LORA), cache_spec_t(MLA_ROPE), cache_spec_t(LANES), cache_spec_t(LANES),
                     _layer_spec((MLA_KV_LORA, KV_UP_COLS), layer),
                     _const_spec((LANES, MLA_ROPE))]
        args += [c_ckv, c_kpe, c_k, c_v, pw["w_kv_up"], pw["kpe_place"]]
        scratch = [pltpu.VMEM((past, MLA_HEADS * LANES), BF16), pltpu.VMEM((past, MLA_HEADS * MLA_V), BF16),
                   pltpu.VMEM((LANES, past), BF16), pltpu.VMEM((LANES, past), BF16)]
    n = batch * seq
    return pl.pallas_call(
        functools.partial(_attn_kernel, latent=latent, tq=tq, sub=sub, seq=seq, layer=layer),
        out_shape=[jax.ShapeDtypeStruct((n, MLA_HEADS * MLA_V), BF16),
                   jax.ShapeDtypeStruct((n, 2 * LANES), BF16)],
        grid=grid,
        in_specs=in_specs,
        out_specs=[q_spec(MLA_HEADS * MLA_V), q_spec(2 * LANES)],
        scratch_shapes=scratch,
        compiler_params=pltpu.CompilerParams(
            dimension_semantics=("arbitrary", "arbitrary"), vmem_limit_bytes=VMEM_LIMIT),
        name="attn_lat" if latent else "attn_ctx",
    )(*args)


def _back_kernel(x_ref, mod_ref, ab_ref, pc_ref, pprev_ref, pnext_ref, ysgu_ref, ym_ref, ys_ref, convw_ref,
                 wout_ref, n2_ref, w1_ref, w2_ref, fn_ref, o_ref, *, seq, sub, final, hidden_chunk):
    t = x_ref.shape[0]
    row = pl.program_id(0) * t + lax.broadcasted_iota(jnp.int32, (t, GROUP_W), 0)
    local = lax.broadcasted_iota(jnp.int32, (t, GROUP_W), 0)
    pc = pc_ref[...]
    prev_row = pprev_ref[SUBLANES - 1:SUBLANES, :]
    next_row = pnext_ref[0:1, :]
    up = jnp.where(local == 0, prev_row, pltpu.roll(pc, 1, 0))
    up = jnp.where(row % seq == 0, 0.0, up)
    dn = jnp.where(local == t - 1, next_row, pltpu.roll(pc, t - 1, 0))
    dn = jnp.where((row + 1) % seq == 0, 0.0, dn)
    conv = convw_ref[0:1, :] * up + convw_ref[1:2, :] * pc + convw_ref[2:3, :] * dn
    y_conv = (ab_ref[...] * conv).astype(BF16)

    y = jnp.concatenate([y_conv, ysgu_ref[...], ym_ref[...], ys_ref[...]], axis=1)
    gate1 = mod_ref[:, 2 * D_MODEL:3 * D_MODEL]
    shift2 = mod_ref[:, 3 * D_MODEL:4 * D_MODEL]
    scale2 = mod_ref[:, 4 * D_MODEL:5 * D_MODEL]
    gate2 = mod_ref[:, 5 * D_MODEL:6 * D_MODEL]
    subs = [slice(k * sub, (k + 1) * sub) for k in range(t // sub)]
    x1s = [x_ref[r, :] + gate1 * _dot(y[r], wout_ref[...]) for r in subs]
    for r, x1 in zip(subs, x1s):
        h2 = (_rms(x1, n2_ref[...]) * (1.0 + scale2) + shift2).astype(BF16)
        acc = jnp.zeros((sub, D_MODEL), F32)
        for c in range(MLP_HIDDEN // hidden_chunk):
            cols = slice(c * hidden_chunk, (c + 1) * hidden_chunk)
            hid = jnp.maximum(_dot(h2, w1_ref[:, cols]), 0.0)
            acc = acc + _dot((hid * hid).astype(BF16), w2_ref[cols, :])
        x2 = x1 + gate2 * acc
        if final:
            x2 = _rms(x2, fn_ref[...])
        o_ref[r, :] = x2


def _back_call(x, mod, ab, pc, ysgu, ym, ys, pw, final_norm, *, layer, seq, tile, sub, mod_row0, per_seq_mod,
               final):
    n = x.shape[0]
    halo_blocks = n // SUBLANES
    per_tile = tile // SUBLANES

    def row_spec(width):
        return pl.BlockSpec((tile, width), lambda i: (i, 0))

    def mod_row(i):
        return mod_row0 + (i * tile) // seq if per_seq_mod else mod_row0

    in_specs = [
        row_spec(D_MODEL),
        _mod_spec(N_MOD * D_MODEL, 0, layer, mod_row),
        row_spec(GROUP_W), row_spec(GROUP_W),
        pl.BlockSpec((SUBLANES, GROUP_W), lambda i: (jnp.maximum(i * per_tile - 1, 0), 0)),
        pl.BlockSpec((SUBLANES, GROUP_W), lambda i: (jnp.minimum((i + 1) * per_tile, halo_blocks - 1), 0)),
        row_spec(GROUP_W), row_spec(GROUP_W), row_spec(GROUP_W),
        _layer_spec((CONV_WIDTH, GROUP_W), layer),
        _layer_spec((D_MODEL, D_MODEL), layer),
        _layer_spec((1, D_MODEL), layer),
        _layer_spec((D_MODEL, MLP_HIDDEN), layer),
        _layer_spec((MLP_HIDDEN, D_MODEL), layer),
        _const_spec((1, D_MODEL)),
    ]
    return pl.pallas_call(
        functools.partial(_back_kernel, seq=seq, sub=sub, final=final, hidden_chunk=HIDDEN_CHUNK),
        out_shape=jax.ShapeDtypeStruct((n, D_MODEL), F32),
        grid=(n // tile,),
        in_specs=in_specs,
        out_specs=row_spec(D_MODEL),
        compiler_params=pltpu.CompilerParams(
            dimension_semantics=("arbitrary",), vmem_limit_bytes=VMEM_LIMIT),
        name="back",
    )(x, mod, ab, pc, pc, pc, ysgu, ym, ys, pw["conv_w"], pw["w_out"], pw["norm2"], pw["w1"], pw["w2"],
      final_norm)


def _rope_tables(n_tokens, rot_dim):
    rows = n_tokens // GRID_W
    row = jnp.repeat(jnp.arange(rows, dtype=F32), GRID_W)
    col = jnp.tile(jnp.arange(GRID_W, dtype=F32), rows)
    half = rot_dim // 2
    inv_freq = ROPE_THETA ** (-jnp.arange(0, half, 2, dtype=F32) / half)
    ang_r = row[:, None] * inv_freq[None, :]
    ang_c = col[:, None] * inv_freq[None, :]
    ang = jnp.concatenate([ang_r, ang_r, ang_c, ang_c], axis=-1)
    sign = jnp.where((jnp.arange(rot_dim) & (rot_dim // 4)) == 0, -1.0, 1.0).astype(F32)
    return jnp.cos(ang), jnp.sin(ang) * sign[None, :]


def _lane_tables(n_tokens):
    cos_s, sin_s = _rope_tables(n_tokens, HEAD_DIM)
    cos_s = jnp.tile(cos_s, (1, LANES // HEAD_DIM))
    sin_s = jnp.tile(sin_s, (1, LANES // HEAD_DIM))
    cos_m, sin_m = _rope_tables(n_tokens, MLA_ROPE)
    pad = LANES - MLA_ROPE
    cos_m = jnp.concatenate([cos_m, jnp.ones((n_tokens, pad), F32)], axis=1)
    sin_m = jnp.concatenate([sin_m, jnp.zeros((n_tokens, pad), F32)], axis=1)
    return cos_s, sin_s, cos_m, sin_m


def _prep_weights(norm1, norm2, w_in, conv_w, sgu_norm, sgu_w, sgu_b, mla_q_norm, mla_w_q_up, mla_kv_norm,
                  mla_w_kv_up, w_out, mlp_w1, mlp_w2):
    depth, d, _ = w_in.shape
    s_kpe = IN_MAIN + MLA_KV_LORA
    s_q = s_kpe + MLA_ROPE
    w_in_main = w_in.astype(BF16)
    sq = w_in[:, :, s_q:s_q + 4 * HEAD_DIM].reshape(depth, d, SWA_KV_HEADS, SWA_GROUP, HEAD_DIM)
    sq = sq.transpose(0, 1, 3, 2, 4).reshape(depth, d, 4 * HEAD_DIM)
    w_in_tail = jnp.concatenate(
        [w_in[:, :, IN_MAIN:s_q], jnp.zeros((depth, d, LANES - MLA_ROPE), F32), sq,
         w_in[:, :, s_q + 4 * HEAD_DIM:]], axis=2).astype(BF16)

    wq = mla_w_q_up.reshape(depth, MLA_Q_LORA, MLA_HEADS, MLA_NOPE + MLA_ROPE)
    wq_p = jnp.concatenate(
        [wq[..., MLA_NOPE:], wq[..., :MLA_NOPE],
         jnp.zeros((depth, MLA_Q_LORA, MLA_HEADS, LANES - MLA_NOPE - MLA_ROPE), F32)], axis=-1)
    wq_p = wq_p.reshape(depth, MLA_Q_LORA, MLA_HEADS * LANES).astype(BF16)

    wkv = mla_w_kv_up.reshape(depth, MLA_KV_LORA, MLA_HEADS, MLA_NOPE + MLA_V)
    wk_p = jnp.concatenate(
        [jnp.zeros((depth, MLA_KV_LORA, MLA_HEADS, MLA_ROPE), F32), wkv[..., :MLA_NOPE],
         jnp.zeros((depth, MLA_KV_LORA, MLA_HEADS, LANES - MLA_NOPE - MLA_ROPE), F32)], axis=-1)
    wkv_p = jnp.concatenate(
        [wk_p.reshape(depth, MLA_KV_LORA, MLA_HEADS * LANES),
         wkv[..., MLA_NOPE:].reshape(depth, MLA_KV_LORA, MLA_HEADS * MLA_V)], axis=2).astype(BF16)

    sguw_p = sgu_w.transpose(0, 2, 1, 3).reshape(depth, CHUNK, SGU_HEADS * CHUNK).astype(BF16)
    sgub_p = jnp.repeat(sgu_b.transpose(0, 2, 1), HEAD_DIM, axis=2)

    wo_swa = w_out[:, 3 * GROUP_W:].reshape(depth, SWA_KV_HEADS, SWA_GROUP, HEAD_DIM, D_MODEL)
    wo_swa = wo_swa.transpose(0, 2, 1, 3, 4).reshape(depth, GROUP_W, D_MODEL)
    wo_p = jnp.concatenate([w_out[:, :3 * GROUP_W], wo_swa], axis=1).astype(BF16)

    return {
        "norm1": norm1[:, None, :], "norm2": norm2[:, None, :], "w_in": w_in_main, "w_in_tail": w_in_tail,
        "conv_w": conv_w,
        "sgu_norm": sgu_norm[:, None, :], "sgu_w": sguw_p, "sgu_b": sgub_p,
        "q_norm": mla_q_norm[:, None, :], "w_q_up": wq_p, "kv_norm": mla_kv_norm[:, None, :],
        "w_kv_up": wkv_p, "kpe_place": jnp.eye(LANES, MLA_ROPE, dtype=F32), "w_out": wo_p,
        "w1": mlp_w1.astype(BF16), "w2": mlp_w2.astype(BF16),
    }


def _feature_major(cache):
    b, depth, past, heads, dim = cache.shape
    return cache.transpose(0, 1, 3, 4, 2).reshape(b, depth, heads * dim, past)


def _pick_tile(n, pref):
    while n % pref:
        pref //= 2
    return pref


def kernel(x_prompt, x_sample, cache_mla_ckv, cache_mla_kpe, cache_swa_k, cache_swa_v, c, c_ctx, w_ada, b_ada,
           norm1, norm2, w_in, conv_w, sgu_norm, sgu_w, sgu_b, mla_q_norm, mla_w_q_up, mla_kv_norm, mla_w_kv_up,
           swa_sink, w_out, mlp_w1, mlp_w2, final_norm):
    batch, seq, d = x_prompt.shape
    dec_batch, dec_seq, _ = x_sample.shape
    depth = w_ada.shape[0]
    past = cache_mla_ckv.shape[2]
    assert d == D_MODEL and seq % CHUNK == 0 and dec_seq % (2 * CHUNK) == 0 and 1 + dec_batch <= MOD_ROWS

    c_all = jnp.concatenate(
        [c_ctx[None, :], c, jnp.zeros((MOD_ROWS - 1 - dec_batch, d), F32)], axis=0)
    mod = _ada_call(c_all, w_ada, b_ada).reshape(depth, MOD_ROWS, 1, N_MOD * d)

    pw = _prep_weights(norm1, norm2, w_in, conv_w, sgu_norm, sgu_w, sgu_b, mla_q_norm, mla_w_q_up, mla_kv_norm,
                       mla_w_kv_up, w_out, mlp_w1, mlp_w2)
    tables = _lane_tables(dec_seq)
    cache = (cache_mla_ckv, jnp.swapaxes(cache_mla_kpe, 2, 3),
             _feature_major(cache_swa_k), _feature_major(cache_swa_v))
    fn = final_norm[None, :]

    xp = x_prompt.reshape(batch * seq, d)
    xs = x_sample.reshape(dec_batch * dec_seq, d)
    tile_p = _pick_tile(batch * seq, ROW_SUB)
    tile_s = _pick_tile(dec_seq, ROW_SUB)
    tq_s = _pick_tile(dec_seq, QUERY_SUB)
    ftile_p = _pick_tile(batch * seq, 2 * tile_p)
    ftile_s = _pick_tile(dec_seq, 2 * tile_s)
    atile_s = _pick_tile(dec_seq, 2 * tq_s)
    atile_p = _pick_tile(batch * seq, CTX_SEQS_PER_STEP * seq)
    bsub = _pick_tile(min(ftile_p, ftile_s), BACK_SUB)
    new_cache = [jax.ShapeDtypeStruct((batch, depth) + slab, F32)
                 for slab in ((seq, MLA_KV_LORA), (MLA_ROPE, seq), (LANES, seq), (LANES, seq))]
    for l in range(depth):
        final = l == depth - 1

        fo = _front_call(xp, mod, pw, None, new_cache, layer=l, seq=seq, tile=ftile_p, sub=tile_p, mod_row0=0)
        ab, pc, ysgu = fo[:3]
        new_cache = list(fo[9:13])
        ym, ys = _attn_call(fo[3:9], swa_sink, pw, None, batch=batch, seq=seq, tq=atile_p, sub=seq, layer=l)
        xp = _back_call(xp, mod, ab, pc, ysgu, ym, ys, pw, fn, layer=l, seq=seq, tile=ftile_p, sub=bsub,
                        mod_row0=0, per_seq_mod=False, final=final)

        fo = _front_call(xs, mod, pw, tables, None, layer=l, seq=dec_seq, tile=ftile_s, sub=tile_s, mod_row0=1)
        ab, pc, ysgu = fo[:3]
        ym, ys = _attn_call(fo[3:9], swa_sink, pw, cache, batch=dec_batch, seq=dec_seq, tq=atile_s, sub=tq_s,
                            layer=l)
        xs = _back_call(xs, mod, ab, pc, ysgu, ym, ys, pw, fn, layer=l, seq=dec_seq, tile=ftile_s, sub=bsub,
                        mod_row0=1, per_seq_mod=True, final=final)

    def token_major(t):
        return t.reshape(batch, depth, SWA_KV_HEADS, HEAD_DIM, seq).transpose(0, 1, 4, 2, 3)

    return (xp.reshape(batch, seq, d), xs.reshape(dec_batch, dec_seq, d), new_cache[0],
            jnp.swapaxes(new_cache[1], 2, 3), token_major(new_cache[2]), token_major(new_cache[3]))
```

```python
import functools

import jax
import jax.numpy as jnp
from jax import lax
from jax.experimental import pallas as pl
from jax.experimental.pallas import tpu as pltpu

D_MODEL = 1024
GROUP_W = 256
HEAD_DIM = 64
CONV_WIDTH = 3
CHUNK = 128
SGU_HEADS = 4
MLA_HEADS = 4
MLA_NOPE = 64
MLA_ROPE = 32
MLA_V = 64
MLA_Q_LORA = 256
MLA_KV_LORA = 128
SWA_KV_HEADS = 2
SWA_GROUP = 2
SWA_WINDOW = 128
MLP_HIDDEN = 4096
GRID_W = 64
ROPE_THETA = 10000.0
EPS = 1e-6
N_MOD = 6
MLA_SCALE = (MLA_NOPE + MLA_ROPE) ** -0.5
SWA_SCALE = HEAD_DIM ** -0.5
NEG_INF = -1e30
LOG2E = 1.4426950408889634

LANES = 128
SUBLANES = 8
MOD_ROWS = 16
VMEM_LIMIT = 56 * 1024 * 1024
ROW_SUB = 512
QUERY_SUB = 256
BACK_SUB = 256
CTX_SEQS_PER_STEP = 8
ADA_COLS = 1536
HIDDEN_CHUNK = 1024

C_CONV = 0
C_SGU = 768
C_CQ = 1280
IN_MAIN = 1536
T_KV = 0
T_SWA = 256
IN_TAIL = 768
KV_UP_COLS = MLA_HEADS * LANES + MLA_HEADS * MLA_V

BF16 = jnp.bfloat16
F32 = jnp.float32


def _dot(a, b):
    return jnp.dot(a, b, preferred_element_type=F32)


def _dot_nt(a, b):
    return lax.dot_general(a, b, (((1,), (1,)), ((), ())), preferred_element_type=F32)


def _rms(x, g):
    return x * lax.rsqrt(jnp.mean(x * x, axis=-1, keepdims=True) + EPS) * g


def _rope(x, cos, sin_signed, quarter):
    lane = lax.broadcasted_iota(jnp.int32, x.shape, 1)
    first = (lane & quarter) == 0
    rot = jnp.where(first, pltpu.roll(x, LANES - quarter, 1), pltpu.roll(x, quarter, 1))
    return x * cos + rot * sin_signed


def _layer_spec(block, layer):
    nd = len(block)
    return pl.BlockSpec((None,) + tuple(block), lambda *_: (layer,) + (0,) * nd, pipeline_mode=pl.Buffered(1))


def _const_spec(shape):
    nd = len(shape)
    return pl.BlockSpec(shape, lambda *_: (0,) * nd, pipeline_mode=pl.Buffered(1))


def _mod_spec(width, col_block, layer, row_of_step):
    return pl.BlockSpec((None, None, 1, width), lambda *idx: (layer, row_of_step(*idx), 0, col_block))


def _ada_kernel(c_ref, w_ref, b_ref, o_ref):
    c = c_ref[...]
    s = (c * jax.nn.sigmoid(c)).astype(BF16)
    o_ref[...] = _dot(s, w_ref[...].astype(BF16)) + b_ref[...]


def _ada_call(c_all, w_ada, b_ada):
    depth = w_ada.shape[0]
    tn = ADA_COLS
    n_out = N_MOD * D_MODEL
    return pl.pallas_call(
        _ada_kernel,
        out_shape=jax.ShapeDtypeStruct((depth, MOD_ROWS, n_out), F32),
        grid=(depth, n_out // tn),
        in_specs=[
            pl.BlockSpec((MOD_ROWS, D_MODEL), lambda l, j: (0, 0)),
            pl.BlockSpec((None, D_MODEL, tn), lambda l, j: (l, 0, j)),
            pl.BlockSpec((None, 1, tn), lambda l, j: (l, 0, j)),
        ],
        out_specs=pl.BlockSpec((None, MOD_ROWS, tn), lambda l, j: (l, 0, j)),
        compiler_params=pltpu.CompilerParams(
            dimension_semantics=("arbitrary", "arbitrary"), vmem_limit_bytes=VMEM_LIMIT),
        name="ada_mod",
    )(c_all, w_ada, b_ada.reshape(depth, 1, n_out))


def _front_kernel(*refs, rope, cache_mode, sub):
    (x_ref, mod_ref, n1_ref, win_ref, wtail_ref, sgun_ref, sguw_ref, sgub_ref, qn_ref, wq_ref, kvn_ref,
     wkv_ref) = refs[:12]
    pos = 12
    if rope:
        cos_s_ref, sin_s_ref, cos_m_ref, sin_m_ref = refs[pos:pos + 4]
        pos += 4
    cache_out = cache_mode is not None
    if cache_mode == "update":
        pos += 4
    (ab_ref, pc_ref, ysgu_ref, qm_ref, kcat_ref, vm_ref, qs_ref, ks_ref, vs_ref) = refs[pos:pos + 9]
    pos += 9
    if cache_out:
        ckv_o, kpe_o, ksw_o, vsw_o = refs[pos:pos + 4]
        if cache_mode == "init":
            for o_ref in (ckv_o, kpe_o, ksw_o, vsw_o):
                o_ref[:, 1:] = jnp.zeros((o_ref.shape[0], o_ref.shape[1] - 1) + o_ref.shape[2:], F32)

        def put(o_ref, val, st):
            per = val.shape[0] // o_ref.shape[-2]
            slab = val.reshape((per,) + o_ref.shape[-2:])
            if cache_mode == "init":
                o_ref[st * per:(st + 1) * per, 0] = slab
            else:
                o_ref[st * per:(st + 1) * per] = slab

        def put_t(o_ref, val, st):
            width, seq_len = o_ref.shape[-2:]
            per = val.shape[0] // seq_len
            val_t = val.T
            for s in range(per):
                slab = val_t[:width, s * seq_len:(s + 1) * seq_len]
                if cache_mode == "init":
                    o_ref[st * per + s, 0] = slab
                else:
                    o_ref[st * per + s] = slab

    shift = mod_ref[:, 0:D_MODEL]
    scale = mod_ref[:, D_MODEL:2 * D_MODEL]
    head_of_lane = lax.broadcasted_iota(jnp.int32, (CHUNK, GROUP_W), 1) // HEAD_DIM

    for st in range(x_ref.shape[0] // sub):
        rows = slice(st * sub, (st + 1) * sub)
        h = (_rms(x_ref[rows, :], n1_ref[...]) * (1.0 + scale) + shift).astype(BF16)
        if rope:
            cos_s, sin_s = cos_s_ref[rows, :], sin_s_ref[rows, :]
            cos_m, sin_m = cos_m_ref[rows, :], sin_m_ref[rows, :]

        zq = _dot(h, win_ref[:, C_CQ:C_CQ + MLA_Q_LORA])
        zkv = _dot(h, wtail_ref[:, T_KV:T_KV + 2 * LANES])
        zs = _dot(h, win_ref[:, C_SGU:C_SGU + 2 * GROUP_W])

        cqn = _rms(zq, qn_ref[...]).astype(BF16)
        q = _dot(cqn, wq_ref[...]) * (MLA_SCALE * LOG2E)
        for hh in range(MLA_HEADS):
            qh = q[:, hh * LANES:(hh + 1) * LANES]
            if rope:
                qh = _rope(qh, cos_m, sin_m, MLA_ROPE // 4)
            qm_ref[rows, hh * LANES:(hh + 1) * LANES] = qh.astype(BF16)

        ckv = _rms(zkv[:, :MLA_KV_LORA], kvn_ref[...])
        kpe = zkv[:, MLA_KV_LORA:]
        if cache_out:
            put(ckv_o, ckv, st)
            put_t(kpe_o, kpe, st)
        if rope:
            kpe = _rope(kpe, cos_m, sin_m, MLA_ROPE // 4)
        kv = _dot(ckv.astype(BF16), wkv_ref[...])
        for hh in range(MLA_HEADS):
            kcat_ref[rows, hh * LANES:(hh + 1) * LANES] = (kv[:, hh * LANES:(hh + 1) * LANES] + kpe).astype(BF16)
        vm_ref[rows, :] = kv[:, MLA_HEADS * LANES:].astype(BF16)

        zc = _dot(h, win_ref[:, C_CONV:C_CONV + 3 * GROUP_W])
        ab_ref[rows, :] = zc[:, :GROUP_W]
        pc_ref[rows, :] = zc[:, GROUP_W:2 * GROUP_W] * zc[:, 2 * GROUP_W:]

        zw = _dot(h, wtail_ref[:, T_SWA:T_SWA + 4 * LANES])
        sk = zw[:, 2 * LANES:3 * LANES]
        sv_ = zw[:, 3 * LANES:]
        if cache_out:
            put_t(ksw_o, sk, st)
            put_t(vsw_o, sv_, st)
        for g in range(SWA_GROUP):
            qg = zw[:, g * LANES:(g + 1) * LANES] * (SWA_SCALE * LOG2E)
            if rope:
                qg = _rope(qg, cos_s, sin_s, HEAD_DIM // 4)
            qs_ref[rows, g * LANES:(g + 1) * LANES] = qg.astype(BF16)
        if rope:
            sk = _rope(sk, cos_s, sin_s, HEAD_DIM // 4)
        ks_ref[rows, :] = sk.astype(BF16)
        vs_ref[rows, :] = sv_.astype(BF16)

        gz = jax.nn.gelu(zs, approximate=True)
        u = gz[:, :GROUP_W]
        vn = _rms(gz[:, GROUP_W:], sgun_ref[...])
        for c in range(sub // CHUNK):
            crow = slice(c * CHUNK, (c + 1) * CHUNK)
            vc = vn[crow]
            stacked = jnp.concatenate(
                [jnp.where(head_of_lane == hh, vc, 0.0).astype(BF16) for hh in range(SGU_HEADS)], axis=0)
            sv = _dot(sguw_ref[...], stacked) + sgub_ref[...]
            ysgu_ref[st * sub + c * CHUNK:st * sub + (c + 1) * CHUNK, :] = (u[crow] * sv).astype(BF16)


def _front_call(x, mod, pw, tables, cache_stacks, *, layer, seq, tile, sub, mod_row0):
    n = x.shape[0]
    rope = tables is not None
    tiles_per_seq = max(seq // tile, 1)

    def row_spec(width):
        return pl.BlockSpec((tile, width), lambda i: (i, 0))

    def mod_row(i):
        return mod_row0 + (i * tile) // seq if rope else mod_row0

    in_specs = [
        row_spec(D_MODEL),
        _mod_spec(2 * D_MODEL, 0, layer, mod_row),
        _layer_spec((1, D_MODEL), layer),
        _layer_spec((D_MODEL, IN_MAIN), layer),
        _layer_spec((D_MODEL, IN_TAIL), layer),
        _layer_spec((1, GROUP_W), layer),
        _layer_spec((CHUNK, SGU_HEADS * CHUNK), layer),
        _layer_spec((CHUNK, GROUP_W), layer),
        _layer_spec((1, MLA_Q_LORA), layer),
        _layer_spec((MLA_Q_LORA, MLA_HEADS * LANES), layer),
        _layer_spec((1, MLA_KV_LORA), layer),
        _layer_spec((MLA_KV_LORA, KV_UP_COLS), layer),
    ]
    args = [x, mod, pw["norm1"], pw["w_in"], pw["w_in_tail"], pw["sgu_norm"], pw["sgu_w"], pw["sgu_b"],
            pw["q_norm"], pw["w_q_up"], pw["kv_norm"], pw["w_kv_up"]]
    if rope:
        in_specs += [pl.BlockSpec((tile, LANES), lambda i: (i % tiles_per_seq, 0))] * 4
        args += list(tables)

    out_shape = [
        jax.ShapeDtypeStruct((n, GROUP_W), F32),
        jax.ShapeDtypeStruct((n, GROUP_W), F32),
        jax.ShapeDtypeStruct((n, GROUP_W), BF16),
        jax.ShapeDtypeStruct((n, MLA_HEADS * LANES), BF16),
        jax.ShapeDtypeStruct((n, MLA_HEADS * LANES), BF16),
        jax.ShapeDtypeStruct((n, MLA_HEADS * MLA_V), BF16),
        jax.ShapeDtypeStruct((n, 2 * LANES), BF16),
        jax.ShapeDtypeStruct((n, LANES), BF16),
        jax.ShapeDtypeStruct((n, LANES), BF16),
    ]
    out_specs = [row_spec(s.shape[1]) for s in out_shape]
    aliases = {}
    cache_mode = None
    if cache_stacks is not None:
        assert tile % seq == 0
        cache_mode = "init" if layer == 0 else "update"
        for stack in cache_stacks:
            depth, slab = stack.shape[1], tuple(stack.shape[2:])
            out_shape.append(jax.ShapeDtypeStruct(stack.shape, stack.dtype))
            if cache_mode == "init":
                out_specs.append(pl.BlockSpec((tile // seq, depth) + slab, lambda i: (i, 0, 0, 0)))
            else:
                aliases[len(args)] = len(out_shape) - 1
                in_specs.append(pl.BlockSpec(memory_space=pl.ANY))
                args.append(stack)
                out_specs.append(pl.BlockSpec((tile // seq, None) + slab, lambda i: (i, layer, 0, 0)))
    return pl.pallas_call(
        functools.partial(_front_kernel, rope=rope, cache_mode=cache_mode, sub=sub),
        out_shape=out_shape,
        grid=(n // tile,),
        in_specs=in_specs,
        out_specs=out_specs,
        input_output_aliases=aliases,
        compiler_params=pltpu.CompilerParams(
            dimension_semantics=("arbitrary",), vmem_limit_bytes=VMEM_LIMIT),
        name="front_lat" if rope else "front_ctx",
    )(*args)


def _scores(q, pieces):
    scores = []
    for k, _, mask, transposed in pieces:
        s = _dot(q, k) if transposed else _dot_nt(q, k)
        if mask is not None:
            s = jnp.where(mask, s, NEG_INF)
        scores.append(s)
    return scores


def _softmax(scores, sink):
    m = functools.reduce(jnp.maximum, [jnp.max(s, axis=-1, keepdims=True) for s in scores])
    if sink is not None:
        m = jnp.maximum(m, sink)
    denom = jnp.exp2(sink - m) if sink is not None else jnp.zeros_like(m)
    probs = []
    for s in scores:
        p = jnp.exp2(s - m)
        denom = denom + jnp.sum(p, axis=-1, keepdims=True)
        probs.append(p.astype(BF16))
    return probs, 1.0 / denom


def _pv(probs, inv_denom, pieces):
    out = None
    for p, (_, v, _, transposed) in zip(probs, pieces):
        pv = _dot_nt(p, v) if transposed else _dot(p, v)
        out = pv if out is None else out + pv
    return out * inv_denom


def _attn_kernel(*refs, latent, tq, sub, seq, layer):
    sink_ref, qm_ref, kcat_ref, vm_ref, qs_ref, ks_ref, vs_ref = refs[:7]
    pos = 7
    if latent:
        cckv_ref, ckpe_ref, cks_ref, cvs_ref, wkv_ref, place_ref = refs[pos:pos + 6]
        pos += 6
    ym_ref, ys_ref = refs[pos:pos + 2]
    pos += 2
    if latent:
        kc_scr, vc_scr, ksc_scr, vsc_scr = refs[pos:pos + 4]
        qi = pl.program_id(1)

        @pl.when(qi == 0)
        def _expand_cache():
            kv = _dot(cckv_ref[...].astype(BF16), wkv_ref[...])
            kp = _dot(place_ref[...], ckpe_ref[...]).T
            for hh in range(MLA_HEADS):
                kc_scr[:, hh * LANES:(hh + 1) * LANES] = (kv[:, hh * LANES:(hh + 1) * LANES] + kp).astype(BF16)
            vc_scr[...] = kv[:, MLA_HEADS * LANES:].astype(BF16)
            ksc_scr[...] = cks_ref[...].astype(BF16)
            vsc_scr[...] = cvs_ref[...].astype(BF16)

    tasks = []
    kv_of_lane = lax.broadcasted_iota(jnp.int32, (sub, LANES), 1) // HEAD_DIM
    for sb in range(tq // sub):
        qrows = slice(sb * sub, (sb + 1) * sub)
        krows = slice(None) if latent else qrows
        for hh in range(MLA_HEADS):
            cols = slice(hh * LANES, (hh + 1) * LANES)
            pieces = [(kcat_ref[krows, cols], vm_ref[krows, :], None, False)]
            if latent:
                pieces.append((kc_scr[:, cols], vc_scr[...], None, False))
            tasks.append((qm_ref[qrows, cols], pieces, None))
        if latent:
            win = sub + 2 * SWA_WINDOW
            q0 = qi * tq + sb * sub
            start = pl.multiple_of(jnp.clip(q0 - SWA_WINDOW, 0, seq - win), SWA_WINDOW)
            q_pos = q0 + lax.broadcasted_iota(jnp.int32, (sub, win), 0)
            k_pos = start + lax.broadcasted_iota(jnp.int32, (sub, win), 1)
            band_ok = jnp.abs(q_pos - k_pos) <= SWA_WINDOW
            swa_pieces = [(ks_ref[pl.ds(start, win), :], vs_ref[pl.ds(start, win), :], band_ok, False),
                          (ksc_scr[...], vsc_scr[...], None, True)]
        else:
            swa_pieces = [(ks_ref[krows, :], vs_ref[krows, :], None, False)]
        for g in range(SWA_GROUP):
            qg = qs_ref[qrows, g * LANES:(g + 1) * LANES].astype(F32)
            for n in range(SWA_KV_HEADS):
                qh = jnp.where(kv_of_lane == n, qg, 0.0).astype(BF16)
                tasks.append((qh, swa_pieces, sink_ref[layer, n * SWA_GROUP + g] * LOG2E))

    per_block = MLA_HEADS + SWA_GROUP * SWA_KV_HEADS
    head_of_lane = lax.broadcasted_iota(jnp.int32, (sub, MLA_HEADS * MLA_V), 1) // MLA_V
    om = jnp.zeros((sub, MLA_HEADS * MLA_V), F32)
    blk = jnp.zeros((sub, LANES), F32)
    scores = _scores(tasks[0][0], tasks[0][1])
    for i, (_, pieces, sink) in enumerate(tasks):
        probs, inv_denom = _softmax(scores, sink)
        scores = _scores(tasks[i + 1][0], tasks[i + 1][1]) if i + 1 < len(tasks) else None
        o = _pv(probs, inv_denom, pieces)
        sb, j = divmod(i, per_block)
        qrows = slice(sb * sub, (sb + 1) * sub)
        if j < MLA_HEADS:
            om = jnp.where(head_of_lane == j, o, om)
            if j == MLA_HEADS - 1:
                ym_ref[qrows, :] = om.astype(BF16)
        else:
            g, n = divmod(j - MLA_HEADS, SWA_KV_HEADS)
            blk = jnp.where(kv_of_lane == n, o, blk)
            if n == SWA_KV_HEADS - 1:
                ys_ref[qrows, g * LANES:(g + 1) * LANES] = blk.astype(BF16)


def _attn_call(front, sink, pw, cache, *, batch, seq, tq, sub, layer):
    qm, kcat, vm, qs, ks, vs = front
    latent = cache is not None
    assert tq % sub == 0 and (seq % tq == 0 if latent else (sub == seq and (batch * seq) % tq == 0))
    n_q = seq // tq if latent else 1
    kv_rows = seq if latent else tq
    grid = (batch, n_q) if latent else (batch * seq // tq, 1)

    def q_spec(width):
        return pl.BlockSpec((tq, width), lambda b, i: (b * n_q + i, 0))

    def seq_spec(width):
        return pl.BlockSpec((kv_rows, width), lambda b, i: (b, 0))

    in_specs = [
        pl.BlockSpec(memory_space=pltpu.SMEM),
        q_spec(MLA_HEADS * LANES), seq_spec(MLA_HEADS * LANES), seq_spec(MLA_HEADS * MLA_V),
        q_spec(2 * LANES), seq_spec(LANES), seq_spec(LANES),
    ]
    args = [sink, qm, kcat, vm, qs, ks, vs]
    scratch = []
    if latent:
        c_ckv, c_kpe, c_k, c_v = cache
        past = c_ckv.shape[2]

        def cache_spec(width):
            return pl.BlockSpec((None, None, past, width), lambda b, i: (b, layer, 0, 0))

        def cache_spec_t(width):
            return pl.BlockSpec((None, None, width, past), lambda b, i: (b, layer, 0, 0))

        in_specs += [cache_spec(MLA_KV_LORA), cache_spec_t(MLA_ROPE), cache_spec_t(LANES), cache_spec_t(LANES),
                     _layer_spec((MLA_KV_LORA, KV_UP_COLS), layer),
                     _const_spec((LANES, MLA_ROPE))]
        args += [c_ckv, c_kpe, c_k, c_v, pw["w_kv_up"], pw["kpe_place"]]
        scratch = [pltpu.VMEM((past, MLA_HEADS * LANES), BF16), pltpu.VMEM((past, MLA_HEADS * MLA_V), BF16),
                   pltpu.VMEM((LANES, past), BF16), pltpu.VMEM((LANES, past), BF16)]
    n = batch * seq
    return pl.pallas_call(
        functools.partial(_attn_kernel, latent=latent, tq=tq, sub=sub, seq=seq, layer=layer),
        out_shape=[jax.ShapeDtypeStruct((n, MLA_HEADS * MLA_V), BF16),
                   jax.ShapeDtypeStruct((n, 2 * LANES), BF16)],
        grid=grid,
        in_specs=in_specs,
        out_specs=[q_spec(MLA_HEADS * MLA_V), q_spec(2 * LANES)],
        scratch_shapes=scratch,
        compiler_params=pltpu.CompilerParams(
            dimension_semantics=("arbitrary", "arbitrary"), vmem_limit_bytes=VMEM_LIMIT),
        name="attn_lat" if latent else "attn_ctx",
    )(*args)


def _back_kernel(x_ref, mod_ref, ab_ref, pc_ref, pprev_ref, pnext_ref, ysgu_ref, ym_ref, ys_ref, convw_ref,
                 wout_ref, n2_ref, w1_ref, w2_ref, fn_ref, o_ref, *, seq, sub, final, hidden_chunk):
    t = x_ref.shape[0]
    row = pl.program_id(0) * t + lax.broadcasted_iota(jnp.int32, (t, GROUP_W), 0)
    local = lax.broadcasted_iota(jnp.int32, (t, GROUP_W), 0)
    pc = pc_ref[...]
    prev_row = pprev_ref[SUBLANES - 1:SUBLANES, :]
    next_row = pnext_ref[0:1, :]
    up = jnp.where(local == 0, prev_row, pltpu.roll(pc, 1, 0))
    up = jnp.where(row % seq == 0, 0.0, up)
    dn = jnp.where(local == t - 1, next_row, pltpu.roll(pc, t - 1, 0))
    dn = jnp.where((row + 1) % seq == 0, 0.0, dn)
    conv = convw_ref[0:1, :] * up + convw_ref[1:2, :] * pc + convw_ref[2:3, :] * dn
    y_conv = (ab_ref[...] * conv).astype(BF16)

    y = jnp.concatenate([y_conv, ysgu_ref[...], ym_ref[...], ys_ref[...]], axis=1)
    gate1 = mod_ref[:, 2 * D_MODEL:3 * D_MODEL]
    shift2 = mod_ref[:, 3 * D_MODEL:4 * D_MODEL]
    scale2 = mod_ref[:, 4 * D_MODEL:5 * D_MODEL]
    gate2 = mod_ref[:, 5 * D_MODEL:6 * D_MODEL]
    subs = [slice(k * sub, (k + 1) * sub) for k in range(t // sub)]
    x1s = [x_ref[r, :] + gate1 * _dot(y[r], wout_ref[...]) for r in subs]
    for r, x1 in zip(subs, x1s):
        h2 = (_rms(x1, n2_ref[...]) * (1.0 + scale2) + shift2).astype(BF16)
        acc = jnp.zeros((sub, D_MODEL), F32)
        for c in range(MLP_HIDDEN // hidden_chunk):
            cols = slice(c * hidden_chunk, (c + 1) * hidden_chunk)
            hid = jnp.maximum(_dot(h2, w1_ref[:, cols]), 0.0)
            acc = acc + _dot((hid * hid).astype(BF16), w2_ref[cols, :])
        x2 = x1 + gate2 * acc
        if final:
            x2 = _rms(x2, fn_ref[...])
        o_ref[r, :] = x2


def _back_call(x, mod, ab, pc, ysgu, ym, ys, pw, final_norm, *, layer, seq, tile, sub, mod_row0, per_seq_mod,
               final):
    n = x.shape[0]
    halo_blocks = n // SUBLANES
    per_tile = tile // SUBLANES

    def row_spec(width):
        return pl.BlockSpec((tile, width), lambda i: (i, 0))

    def mod_row(i):
        return mod_row0 + (i * tile) // seq if per_seq_mod else mod_row0

    in_specs = [
        row_spec(D_MODEL),
        _mod_spec(N_MOD * D_MODEL, 0, layer, mod_row),
        row_spec(GROUP_W), row_spec(GROUP_W),
        pl.BlockSpec((SUBLANES, GROUP_W), lambda i: (jnp.maximum(i * per_tile - 1, 0), 0)),
        pl.BlockSpec((SUBLANES, GROUP_W), lambda i: (jnp.minimum((i + 1) * per_tile, halo_blocks - 1), 0)),
        row_spec(GROUP_W), row_spec(GROUP_W), row_spec(GROUP_W),
        _layer_spec((CONV_WIDTH, GROUP_W), layer),
        _layer_spec((D_MODEL, D_MODEL), layer),
        _layer_spec((1, D_MODEL), layer),
        _layer_spec((D_MODEL, MLP_HIDDEN), layer),
        _layer_spec((MLP_HIDDEN, D_MODEL), layer),
        _const_spec((1, D_MODEL)),
    ]
    return pl.pallas_call(
        functools.partial(_back_kernel, seq=seq, sub=sub, final=final, hidden_chunk=HIDDEN_CHUNK),
        out_shape=jax.ShapeDtypeStruct((n, D_MODEL), F32),
        grid=(n // tile,),
        in_specs=in_specs,
        out_specs=row_spec(D_MODEL),
        compiler_params=pltpu.CompilerParams(
            dimension_semantics=("arbitrary",), vmem_limit_bytes=VMEM_LIMIT),
        name="back",
    )(x, mod, ab, pc, pc, pc, ysgu, ym, ys, pw["conv_w"], pw["w_out"], pw["norm2"], pw["w1"], pw["w2"],
      final_norm)


def _rope_tables(n_tokens, rot_dim):
    rows = n_tokens // GRID_W
    row = jnp.repeat(jnp.arange(rows, dtype=F32), GRID_W)
    col = jnp.tile(jnp.arange(GRID_W, dtype=F32), rows)
    half = rot_dim // 2
    inv_freq = ROPE_THETA ** (-jnp.arange(0, half, 2, dtype=F32) / half)
    ang_r = row[:, None] * inv_freq[None, :]
    ang_c = col[:, None] * inv_freq[None, :]
    ang = jnp.concatenate([ang_r, ang_r, ang_c, ang_c], axis=-1)
    sign = jnp.where((jnp.arange(rot_dim) & (rot_dim // 4)) == 0, -1.0, 1.0).astype(F32)
    return jnp.cos(ang), jnp.sin(ang) * sign[None, :]


def _lane_tables(n_tokens):
    cos_s, sin_s = _rope_tables(n_tokens, HEAD_DIM)
    cos_s = jnp.tile(cos_s, (1, LANES // HEAD_DIM))
    sin_s = jnp.tile(sin_s, (1, LANES // HEAD_DIM))
    cos_m, sin_m = _rope_tables(n_tokens, MLA_ROPE)
    pad = LANES - MLA_ROPE
    cos_m = jnp.concatenate([cos_m, jnp.ones((n_tokens, pad), F32)], axis=1)
    sin_m = jnp.concatenate([sin_m, jnp.zeros((n_tokens, pad), F32)], axis=1)
    return cos_s, sin_s, cos_m, sin_m


def _prep_weights(norm1, norm2, w_in, conv_w, sgu_norm, sgu_w, sgu_b, mla_q_norm, mla_w_q_up, mla_kv_norm,
                  mla_w_kv_up, w_out, mlp_w1, mlp_w2):
    depth, d, _ = w_in.shape
    s_kpe = IN_MAIN + MLA_KV_LORA
    s_q = s_kpe + MLA_ROPE
    w_in_main = w_in.astype(BF16)
    sq = w_in[:, :, s_q:s_q + 4 * HEAD_DIM].reshape(depth, d, SWA_KV_HEADS, SWA_GROUP, HEAD_DIM)
    sq = sq.transpose(0, 1, 3, 2, 4).reshape(depth, d, 4 * HEAD_DIM)
    w_in_tail = jnp.concatenate(
        [w_in[:, :, IN_MAIN:s_q], jnp.zeros((depth, d, LANES - MLA_ROPE), F32), sq,
         w_in[:, :, s_q + 4 * HEAD_DIM:]], axis=2).astype(BF16)

    wq = mla_w_q_up.reshape(depth, MLA_Q_LORA, MLA_HEADS, MLA_NOPE + MLA_ROPE)
    wq_p = jnp.concatenate(
        [wq[..., MLA_NOPE:], wq[..., :MLA_NOPE],
         jnp.zeros((depth, MLA_Q_LORA, MLA_HEADS, LANES - MLA_NOPE - MLA_ROPE), F32)], axis=-1)
    wq_p = wq_p.reshape(depth, MLA_Q_LORA, MLA_HEADS * LANES).astype(BF16)

    wkv = mla_w_kv_up.reshape(depth, MLA_KV_LORA, MLA_HEADS, MLA_NOPE + MLA_V)
    wk_p = jnp.concatenate(
        [jnp.zeros((depth, MLA_KV_LORA, MLA_HEADS, MLA_ROPE), F32), wkv[..., :MLA_NOPE],
         jnp.zeros((depth, MLA_KV_LORA, MLA_HEADS, LANES - MLA_NOPE - MLA_ROPE), F32)], axis=-1)
    wkv_p = jnp.concatenate(
        [wk_p.reshape(depth, MLA_KV_LORA, MLA_HEADS * LANES),
         wkv[..., MLA_NOPE:].reshape(depth, MLA_KV_LORA, MLA_HEADS * MLA_V)], axis=2).astype(BF16)

    sguw_p = sgu_w.transpose(0, 2, 1, 3).reshape(depth, CHUNK, SGU_HEADS * CHUNK).astype(BF16)
    sgub_p = jnp.repeat(sgu_b.transpose(0, 2, 1), HEAD_DIM, axis=2)

    wo_swa = w_out[:, 3 * GROUP_W:].reshape(depth, SWA_KV_HEADS, SWA_GROUP, HEAD_DIM, D_MODEL)
    wo_swa = wo_swa.transpose(0, 2, 1, 3, 4).reshape(depth, GROUP_W, D_MODEL)
    wo_p = jnp.concatenate([w_out[:, :3 * GROUP_W], wo_swa], axis=1).astype(BF16)

    return {
        "norm1": norm1[:, None, :], "norm2": norm2[:, None, :], "w_in": w_in_main, "w_in_tail": w_in_tail,
        "conv_w": conv_w,
        "sgu_norm": sgu_norm[:, None, :], "sgu_w": sguw_p, "sgu_b": sgub_p,
        "q_norm": mla_q_norm[:, None, :], "w_q_up": wq_p, "kv_norm": mla_kv_norm[:, None, :],
        "w_kv_up": wkv_p, "kpe_place": jnp.eye(LANES, MLA_ROPE, dtype=F32), "w_out": wo_p,
        "w1": mlp_w1.astype(BF16), "w2": mlp_w2.astype(BF16),
    }


def _feature_major(cache):
    b, depth, past, heads, dim = cache.shape
    return cache.transpose(0, 1, 3, 4, 2).reshape(b, depth, heads * dim, past)


def _pick_tile(n, pref):
    while n % pref:
        pref //= 2
    return pref


def kernel(x_prompt, x_sample, cache_mla_ckv, cache_mla_kpe, cache_swa_k, cache_swa_v, c, c_ctx, w_ada, b_ada,
           norm1, norm2, w_in, conv_w, sgu_norm, sgu_w, sgu_b, mla_q_norm, mla_w_q_up, mla_kv_norm, mla_w_kv_up,
           swa_sink, w_out, mlp_w1, mlp_w2, final_norm):
    batch, seq, d = x_prompt.shape
    dec_batch, dec_seq, _ = x_sample.shape
    depth = w_ada.shape[0]
    past = cache_mla_ckv.shape[2]
    assert d == D_MODEL and seq % CHUNK == 0 and dec_seq % (2 * CHUNK) == 0 and 1 + dec_batch <= MOD_ROWS

    c_all = jnp.concatenate(
        [c_ctx[None, :], c, jnp.zeros((MOD_ROWS - 1 - dec_batch, d), F32)], axis=0)
    mod = _ada_call(c_all, w_ada, b_ada).reshape(depth, MOD_ROWS, 1, N_MOD * d)

    pw = _prep_weights(norm1, norm2, w_in, conv_w, sgu_norm, sgu_w, sgu_b, mla_q_norm, mla_w_q_up, mla_kv_norm,
                       mla_w_kv_up, w_out, mlp_w1, mlp_w2)
    tables = _lane_tables(dec_seq)
    cache = (cache_mla_ckv, jnp.swapaxes(cache_mla_kpe, 2, 3),
             _feature_major(cache_swa_k), _feature_major(cache_swa_v))
    fn = final_norm[None, :]

    xp = x_prompt.reshape(batch * seq, d)
    xs = x_sample.reshape(dec_batch * dec_seq, d)
    tile_p = _pick_tile(batch * seq, ROW_SUB)
    tile_s = _pick_tile(dec_seq, ROW_SUB)
    tq_s = _pick_tile(dec_seq, QUERY_SUB)
    ftile_p = _pick_tile(batch * seq, 2 * tile_p)
    ftile_s = _pick_tile(dec_seq, 2 * tile_s)
    atile_s = _pick_tile(dec_seq, 2 * tq_s)
    atile_p = _pick_tile(batch * seq, CTX_SEQS_PER_STEP * seq)
    bsub = _pick_tile(min(ftile_p, ftile_s), BACK_SUB)
    new_cache = [jax.ShapeDtypeStruct((batch, depth) + slab, F32)
                 for slab in ((seq, MLA_KV_LORA), (MLA_ROPE, seq), (LANES, seq), (LANES, seq))]
    for l in range(depth):
        final = l == depth - 1

        fo = _front_call(xp, mod, pw, None, new_cache, layer=l, seq=seq, tile=ftile_p, sub=tile_p, mod_row0=0)
        ab, pc, ysgu = fo[:3]
        new_cache = list(fo[9:13])
        ym, ys = _attn_call(fo[3:9], swa_sink, pw, None, batch=batch, seq=seq, tq=atile_p, sub=seq, layer=l)
        xp = _back_call(xp, mod, ab, pc, ysgu, ym, ys, pw, fn, layer=l, seq=seq, tile=ftile_p, sub=bsub,
                        mod_row0=0, per_seq_mod=False, final=final)

        fo = _front_call(xs, mod, pw, tables, None, layer=l, seq=dec_seq, tile=ftile_s, sub=tile_s, mod_row0=1)
        ab, pc, ysgu = fo[:3]
        ym, ys = _attn_call(fo[3:9], swa_sink, pw, cache, batch=dec_batch, seq=dec_seq, tq=atile_s, sub=tq_s,
                            layer=l)
        xs = _back_call(xs, mod, ab, pc, ysgu, ym, ys, pw, fn, layer=l, seq=dec_seq, tile=ftile_s, sub=bsub,
                        mod_row0=1, per_seq_mod=True, final=final)

    def token_major(t):
        return t.reshape(batch, depth, SWA_KV_HEADS, HEAD_DIM, seq).transpose(0, 1, 4, 2, 3)

    return (xp.reshape(batch, seq, d), xs.reshape(dec_batch, dec_seq, d), new_cache[0],
            jnp.swapaxes(new_cache[1], 2, 3), token_major(new_cache[2]), token_major(new_cache[3]))
```
